```python
import math
import jax, jax.numpy as jnp
from jax import lax
import numpy as np

D_MODEL = 1024
BATCH = 32
SEQ = 256
DEPTH = 4
DEC_BATCH = 4
DEC_SEQ = 2048
PAST_LEN = 512

GRID_W = 64
HEAD_DIM = 64
AXIS_DIM = HEAD_DIM // 2
ROPE_THETA = 10000.0
QBLK = 128
GQA_HEADS = 8
GQA_KV_HEADS = 2
DIFF_HEADS = 4
RWKV_HEADS = D_MODEL // HEAD_DIM
DECAY_LORA = 64
AAA_LORA = 64
GATE_LORA = 128
D_FF = 4 * D_MODEL
N_ATTN_LAYERS = (DEPTH + 1) // 2
N_RWKV_LAYERS = DEPTH // 2
GQA_Q = GQA_HEADS * HEAD_DIM
GQA_KV = GQA_KV_HEADS * HEAD_DIM
DIFF_QK = DIFF_HEADS * 2 * HEAD_DIM
DIFF_V = DIFF_HEADS * 2 * HEAD_DIM
ATTN_IN = GQA_Q + 2 * GQA_KV + 2 * DIFF_QK + DIFF_V
ATTN_SPLITS = (GQA_Q, GQA_Q + GQA_KV, GQA_Q + 2 * GQA_KV, GQA_Q + 2 * GQA_KV + DIFF_QK, GQA_Q + 2 * GQA_KV + 2 * DIFF_QK)
MIX_WIDTH = GQA_Q + DIFF_V
NORM_EPS = 1e-6
LNX_EPS = 64e-5

kernel_name = 'hybrid_dit_gqa_diffattn_birwkv7_step'

F32 = jnp.float32


def rms_norm(x, g):
    xf = x.astype(F32)
    y = xf * lax.rsqrt(jnp.mean(xf * xf, axis=-1, keepdims=True) + NORM_EPS)
    return y.astype(x.dtype) * g


def adaln(cond, w, b):
    m = jax.nn.silu(cond) @ w + b
    return jnp.split(m[:, None, :], 6, axis=-1)


def axial_rope(rows):
    row = jnp.repeat(jnp.arange(rows), GRID_W).astype(F32)
    col = jnp.tile(jnp.arange(GRID_W), rows).astype(F32)
    inv = 1.0 / (ROPE_THETA ** (jnp.arange(0, AXIS_DIM, 2, dtype=F32) / AXIS_DIM))
    ar = row[:, None] * inv[None, :]
    ac = col[:, None] * inv[None, :]
    ang = jnp.concatenate([ar, ar, ac, ac], axis=-1)
    return jnp.cos(ang), jnp.sin(ang)


def apply_rope(x, cos, sin):
    bshape = (cos.shape[0],) + (1,) * (x.ndim - 3) + (HEAD_DIM,)
    xh = x.reshape(x.shape[:-1] + (2, 2, AXIS_DIM // 2))
    rot = jnp.stack([-xh[..., 1, :], xh[..., 0, :]], axis=-2).reshape(x.shape)
    return (x.astype(F32) * cos.reshape(bshape) + rot.astype(F32) * sin.reshape(bshape)).astype(x.dtype)


def sweep_query_blocks(fn, q):
    B, T = q.shape[:2]
    nb = T // QBLK
    qb = jnp.moveaxis(q.reshape((B, nb, QBLK) + q.shape[2:]), 1, 0)
    out = lax.map(fn, qb)
    return jnp.moveaxis(out, 0, 1).reshape((B, T) + out.shape[3:])


def gqa_attention(q, k, v):
    B, T = q.shape[:2]
    grp = GQA_HEADS // GQA_KV_HEADS
    qg = q.reshape(B, T, GQA_KV_HEADS, grp, HEAD_DIM)
    scale = HEAD_DIM ** -0.5

    def block(qb):
        s = jnp.einsum('bqhgd,bkhd->bhgqk', qb, k).astype(F32) * scale
        p = jax.nn.softmax(s, axis=-1).astype(v.dtype)
        return jnp.einsum('bhgqk,bkhd->bqhgd', p, v)

    return sweep_query_blocks(block, qg).reshape(B, T, GQA_Q)


def diff_attention(q, k, v, lam):
    scale = HEAD_DIM ** -0.5

    def block(qb):
        s = jnp.einsum('bqhcd,bkhcd->bhcqk', qb, k).astype(F32) * scale
        p = jax.nn.softmax(s, axis=-1)
        pd = (p[:, :, 0] - lam * p[:, :, 1]).astype(v.dtype)
        return jnp.einsum('bhqk,bkhe->bqhe', pd, v)

    return sweep_query_blocks(block, q)


def attn_mixer(h, p, lam_init, rope, ctx):
    w_in, w_out, qk_gain, lam_vec, subln_g = p
    B, T, _ = h.shape
    qa, ka, va, qb, kb, vb = jnp.split(h @ w_in, ATTN_SPLITS, axis=-1)
    qa = rms_norm(qa.reshape(B, T, GQA_HEADS, HEAD_DIM), qk_gain[0])
    ka = rms_norm(ka.reshape(B, T, GQA_KV_HEADS, HEAD_DIM), qk_gain[1])
    va = va.reshape(B, T, GQA_KV_HEADS, HEAD_DIM)
    qb = qb.reshape(B, T, DIFF_HEADS, 2, HEAD_DIM)
    kb = kb.reshape(B, T, DIFF_HEADS, 2, HEAD_DIM)
    vb = vb.reshape(B, T, DIFF_HEADS, 2 * HEAD_DIM)
    side = (ka, va, kb, vb)
    if rope is not None:
        cos, sin = rope
        qa, ka, qb, kb = (apply_rope(t, cos, sin) for t in (qa, ka, qb, kb))
    if ctx is not None:
        ka, va, kb, vb = (jnp.concatenate([cc, t], axis=1) for cc, t in zip(ctx, (ka, va, kb, vb)))
    oa = gqa_attention(qa, ka, va)
    lf = lam_vec.astype(F32)
    lam = jnp.exp(jnp.sum(lf[0] * lf[1])) - jnp.exp(jnp.sum(lf[2] * lf[3])) + lam_init
    ob = rms_norm(diff_attention(qb, kb, vb, lam), subln_g) * (1.0 - lam_init)
    out = jnp.concatenate([oa, ob.reshape(B, T, DIFF_V)], axis=-1) @ w_out
    return out, side


def token_shift_centred(h):
    hp = jnp.pad(h[:, :-1], ((0, 0), (1, 0), (0, 0)))
    hn = jnp.pad(h[:, 1:], ((0, 0), (0, 1), (0, 0)))
    return 0.5 * (hp + hn) - h


def wkv_scan(r, decay, k, v, kk, a, s0, reverse):
    seq = tuple(jnp.moveaxis(t.astype(F32), 1, 0) for t in (r, decay, k, v, kk, kk * a))

    def step(S, inp):
        r_t, w_t, k_t, v_t, kk_t, kka_t = inp
        sk = jnp.einsum('bhvk,bhk->bhv', S, kk_t)
        S = S * w_t[:, :, None, :] - sk[..., None] * kka_t[:, :, None, :] + v_t[..., None] * k_t[:, :, None, :]
        return S, jnp.einsum('bhvk,bhk->bhv', S, r_t)

    sf, ys = lax.scan(step, s0.astype(F32), seq, reverse=reverse)
    return jnp.moveaxis(ys, 0, 1), sf


def rwkv_mixer(h, p, s0):
    mu, w_rkv, w_o, w0, w1, w2, a0, a1, a2, g1, g2, kvec, lnx = p
    B, T, D = h.shape
    hs = (B, T, RWKV_HEADS, HEAD_DIM)
    xx = token_shift_centred(h)
    xr, xw, xk, xv, xa, xg = (h + xx * mu[n] for n in range(6))
    r = (xr @ w_rkv[0]).reshape(hs)
    k = (xk @ w_rkv[1]).reshape(hs)
    v = (xv @ w_rkv[2]).reshape(hs)
    g = jax.nn.sigmoid(xg @ g1) @ g2
    k_k, k_a, r_k = (kvec[n].reshape(RWKV_HEADS, HEAD_DIM) for n in range(3))
    kk = (k * k_k).astype(F32)
    kk = kk * lax.rsqrt(jnp.sum(kk * kk, axis=-1, keepdims=True) + 1e-12)
    ys, bonuses, finals = [], [], []
    for d in range(2):
        wlog = -jax.nn.softplus(-(w0[d] + jnp.tanh(xw @ w1[d]) @ w2[d])) - 0.5
        decay = jnp.exp(-jnp.exp(wlog.astype(F32))).reshape(hs)
        a = jax.nn.sigmoid(a0[d] + (xa @ a1[d]) @ a2[d]).reshape(hs)
        kd = k * (1.0 + (a - 1.0) * k_a)
        y, sf = wkv_scan(r, decay, kd, v, kk, a, s0[:, d], reverse=(d == 1))
        ys.append(y)
        bonuses.append(jnp.sum(r * kd * r_k, axis=-1, keepdims=True) * v)
        finals.append(sf)
    y = ys[0] + ys[1]
    mean = jnp.mean(y, axis=-1, keepdims=True)
    var = jnp.mean(jnp.square(y - mean), axis=-1, keepdims=True)
    y = (y - mean) * lax.rsqrt(var + LNX_EPS)
    y = y.astype(h.dtype) * lnx[0].reshape(RWKV_HEADS, HEAD_DIM) + lnx[1].reshape(RWKV_HEADS, HEAD_DIM)
    y = y + bonuses[0] + bonuses[1]
    out = (y.reshape(B, T, D) * g) @ w_o
    return out, jnp.stack(finals, axis=1).astype(h.dtype)


def trunk_layer(x, cond, w_ada, b_ada, gains, w1, w2, mix_fn):
    sh1, sc1, gt1, sh2, sc2, gt2 = adaln(cond, w_ada, b_ada)
    h = rms_norm(x, gains[0]) * (1.0 + sc1) + sh1
    m, side = mix_fn(h)
    x = x + gt1 * rms_norm(m, gains[1])
    h = rms_norm(x, gains[2]) * (1.0 + sc2) + sh2
    f = jnp.square(jax.nn.relu(h @ w1)) @ w2
    x = x + gt2 * rms_norm(f, gains[3])
    return x, side


def setup_inputs(seed: int = 0) -> dict:
    key = jax.random.key(seed)
    ks = iter(jax.random.split(key, 48))
    D = D_MODEL
    LA, LR = N_ATTN_LAYERS, N_RWKV_LAYERS

    def nrm(shape, scale):
        return scale * jax.random.normal(next(ks), shape, F32)

    def unif(shape, lo, hi):
        return jax.random.uniform(next(ks), shape, F32, lo, hi)

    kv_off = jnp.array([0.85, 1.0, 0.0], F32)[None, :, None]
    kv_sc = jnp.array([0.02, 0.02, 0.1], F32)[None, :, None]
    ln_off = jnp.array([1.0, 0.0], F32)[None, :, None]
    ln_sc = jnp.array([0.02, 0.01], F32)[None, :, None]
    return {
        'x_prompt': nrm((BATCH, SEQ, D), 1.0),
        'x_sample': nrm((DEC_BATCH, DEC_SEQ, D), 1.0),
        'c': nrm((DEC_BATCH, D), 1.0),
        'cache_k_gqa': nrm((DEC_BATCH, LA, PAST_LEN, GQA_KV_HEADS, HEAD_DIM), 1.0),
        'cache_v_gqa': nrm((DEC_BATCH, LA, PAST_LEN, GQA_KV_HEADS, HEAD_DIM), 1.0),
        'cache_k_diff': nrm((DEC_BATCH, LA, PAST_LEN, DIFF_HEADS, 2, HEAD_DIM), 1.0),
        'cache_v_diff': nrm((DEC_BATCH, LA, PAST_LEN, DIFF_HEADS, 2 * HEAD_DIM), 1.0),
        'state_rwkv': nrm((DEC_BATCH, LR, 2, RWKV_HEADS, HEAD_DIM, HEAD_DIM), 0.5),
        'c_ctx': nrm((D,), 1.0),
        'w_ada': nrm((DEPTH, D, 6 * D), 0.5 * D ** -0.5),
        'b_ada': nrm((DEPTH, 6 * D), 0.02),
        'norm_gains': 1.0 + nrm((DEPTH, 4, D), 0.02),
        'attn_w_in': nrm((LA, D, ATTN_IN), D ** -0.5),
        'attn_w_out': nrm((LA, MIX_WIDTH, D), MIX_WIDTH ** -0.5),
        'attn_qk_gain': 1.0 + nrm((LA, 2, HEAD_DIM), 0.02),
        'diff_lambda': nrm((LA, 4, HEAD_DIM), 0.1),
        'diff_subln': 1.0 + nrm((LA, 2 * HEAD_DIM), 0.02),
        'rwkv_mu': unif((LR, 6, D), 0.0, 1.0),
        'rwkv_w_rkv': nrm((LR, 3, D, D), D ** -0.5),
        'rwkv_w_o': nrm((LR, D, D), D ** -0.5),
        'rwkv_w0': unif((LR, 2, D), -5.0, 0.0),
        'rwkv_w1': nrm((LR, 2, D, DECAY_LORA), D ** -0.5),
        'rwkv_w2': nrm((LR, 2, DECAY_LORA, D), 0.1 * DECAY_LORA ** -0.5),
        'rwkv_a0': nrm((LR, 2, D), 0.1),
        'rwkv_a1': nrm((LR, 2, D, AAA_LORA), D ** -0.5),
        'rwkv_a2': nrm((LR, 2, AAA_LORA, D), 0.5 * AAA_LORA ** -0.5),
        'rwkv_g1': nrm((LR, D, GATE_LORA), D ** -0.5),
        'rwkv_g2': nrm((LR, GATE_LORA, D), GATE_LORA ** -0.5),
        'rwkv_kvec': kv_off + kv_sc * jax.random.normal(next(ks), (LR, 3, D), F32),
        'rwkv_lnx': ln_off + ln_sc * jax.random.normal(next(ks), (LR, 2, D), F32),
        'mlp_w1': nrm((DEPTH, D, D_FF), D ** -0.5),
        'mlp_w2': nrm((DEPTH, D_FF, D), D_FF ** -0.5),
    }


def reference(x_prompt, x_sample, c, cache_k_gqa, cache_v_gqa, cache_k_diff, cache_v_diff, state_rwkv,
              c_ctx, w_ada, b_ada, norm_gains, attn_w_in, attn_w_out, attn_qk_gain, diff_lambda, diff_subln,
              rwkv_mu, rwkv_w_rkv, rwkv_w_o, rwkv_w0, rwkv_w1, rwkv_w2, rwkv_a0, rwkv_a1, rwkv_a2,
              rwkv_g1, rwkv_g2, rwkv_kvec, rwkv_lnx, mlp_w1, mlp_w2):
    rows = x_sample.shape[1] // GRID_W
    rope = axial_rope(rows)
    ctx_cond = c_ctx[None, :]
    zero_state = jnp.zeros((x_prompt.shape[0], 2, RWKV_HEADS, HEAD_DIM, HEAD_DIM), x_prompt.dtype)
    xp, xs = x_prompt, x_sample
    kg, vg, kd, vd, st = [], [], [], [], []
    for i in range(DEPTH):
        j = i // 2
        lp = (w_ada[i], b_ada[i], norm_gains[i], mlp_w1[i], mlp_w2[i])
        if i % 2 == 0:
            lam_init = 0.8 - 0.6 * math.exp(-0.3 * i)
            ap = (attn_w_in[j], attn_w_out[j], attn_qk_gain[j], diff_lambda[j], diff_subln[j])
            xp, side = trunk_layer(xp, ctx_cond, *lp, lambda h: attn_mixer(h, ap, lam_init, None, None))
            kg.append(side[0]); vg.append(side[1]); kd.append(side[2]); vd.append(side[3])
            cached = (cache_k_gqa[:, j], cache_v_gqa[:, j], cache_k_diff[:, j], cache_v_diff[:, j])
            xs, _ = trunk_layer(xs, c, *lp, lambda h: attn_mixer(h, ap, lam_init, rope, cached))
        else:
            rp = (rwkv_mu[j], rwkv_w_rkv[j], rwkv_w_o[j], rwkv_w0[j], rwkv_w1[j], rwkv_w2[j],
                  rwkv_a0[j], rwkv_a1[j], rwkv_a2[j], rwkv_g1[j], rwkv_g2[j], rwkv_kvec[j], rwkv_lnx[j])
            xp, sf = trunk_layer(xp, ctx_cond, *lp, lambda h: rwkv_mixer(h, rp, zero_state))
            st.append(sf)
            s0 = state_rwkv[:, j]
            xs, _ = trunk_layer(xs, c, *lp, lambda h: rwkv_mixer(h, rp, s0))
    new_k_gqa = jnp.stack(kg, axis=1)
    new_v_gqa = jnp.stack(vg, axis=1)
    new_k_diff = jnp.stack(kd, axis=1)
    new_v_diff = jnp.stack(vd, axis=1)
    new_state_rwkv = jnp.stack(st, axis=1)
    return (xp, xs, new_k_gqa, new_v_gqa, new_k_diff, new_v_diff, new_state_rwkv)
```

```python
import functools
import math

import numpy as np
import jax
import jax.numpy as jnp
from jax import lax
from jax.experimental import pallas as pl
from jax.experimental.pallas import tpu as pltpu

F32 = jnp.float32
BF16 = jnp.bfloat16

D_MODEL = 1024
DEPTH = 4
GRID_W = 64
HEAD_DIM = 64
AXIS_DIM = HEAD_DIM // 2
ROPE_THETA = 10000.0
GQA_HEADS = 8
GQA_KV_HEADS = 2
DIFF_HEADS = 4
RWKV_HEADS = D_MODEL // HEAD_DIM
D_FF = 4 * D_MODEL
GQA_Q = GQA_HEADS * HEAD_DIM
GQA_KV = GQA_KV_HEADS * HEAD_DIM
DIFF_QK = DIFF_HEADS * 2 * HEAD_DIM
DIFF_V = DIFF_HEADS * 2 * HEAD_DIM
ATTN_IN = GQA_Q + 2 * GQA_KV + 2 * DIFF_QK + DIFF_V
NORM_EPS = 1e-6
LNX_EPS = 64e-5

LANES = 128
N_PAIRS = D_MODEL // LANES
CHUNK = 64
COND_ROWS = 8
VMEM_LIMIT = 56 * 1024 * 1024


def _dot(a, b):
    return jnp.dot(a.astype(BF16), b.astype(BF16), preferred_element_type=F32)


def _dot_nt(a, b):
    return lax.dot_general(a.astype(BF16), b.astype(BF16), (((1,), (1,)), ((), ())), preferred_element_type=F32)


def _dot_tn(a, b):
    return lax.dot_general(a.astype(BF16), b.astype(BF16), (((0,), (0,)), ((), ())), preferred_element_type=F32)


def _rms(x):
    return x * lax.rsqrt(jnp.mean(x * x, axis=-1, keepdims=True) + NORM_EPS)


def _sigmoid(x):
    return 1.0 / (1.0 + jnp.exp(-x))


def _params(*sem):
    return pltpu.CompilerParams(dimension_semantics=sem, vmem_limit_bytes=VMEM_LIMIT)


def _full(shape):
    nd = len(shape)
    return pl.BlockSpec(shape, lambda *_: (0,) * nd)


def _mod_spec(row0, rowstride):
    return pl.BlockSpec((1, 1, 6 * D_MODEL), lambda b, i: (row0 + b * rowstride, 0, 0))


def _adaln_kernel(c_ref, w_ref, b_ref, o_ref):
    c = c_ref[...]
    o_ref[0] = _dot(c * _sigmoid(c), w_ref[0]) + b_ref[0]


def _adaln(cond, w_ada, b_ada):
    tn = 1536
    return pl.pallas_call(
        _adaln_kernel,
        grid=(DEPTH, 6 * D_MODEL // tn),
        in_specs=[_full((COND_ROWS, D_MODEL)),
                  pl.BlockSpec((1, D_MODEL, tn), lambda l, j: (l, 0, j)),
                  pl.BlockSpec((1, 1, tn), lambda l, j: (l, 0, j))],
        out_specs=pl.BlockSpec((1, COND_ROWS, tn), lambda l, j: (l, 0, j)),
        out_shape=jax.ShapeDtypeStruct((DEPTH, COND_ROWS, 6 * D_MODEL), F32),
        compiler_params=_params("parallel", "parallel"),
        name="adaln",
    )(cond, w_ada, b_ada.reshape(DEPTH, 1, 6 * D_MODEL))


def _head_ms(x, ones_bd):
    w = x.shape[-1]
    return _dot(x * x, ones_bd[0:w, 0:w]) * (1.0 / HEAD_DIM)


def _rope(x, cos, sin_a, sin_b):
    cols = []
    for j in range(x.shape[-1] // LANES):
        xj = x[:, j * LANES:(j + 1) * LANES]
        cols.append(xj * cos + pltpu.roll(xj, LANES - AXIS_DIM // 2, 1) * sin_a + pltpu.roll(xj, AXIS_DIM // 2, 1) * sin_b)
    return cols[0] if len(cols) == 1 else jnp.concatenate(cols, axis=-1)


def _attn_in_kernel(*refs, rope):
    if rope:
        x_ref, mod_ref, g_ref, w_ref, gq_ref, gk_ref, ones_ref, cos_ref, sa_ref, sb_ref = refs[:10]
        outs = refs[10:]
    else:
        x_ref, mod_ref, g_ref, w_ref, gq_ref, gk_ref, ones_ref = refs[:7]
        outs = refs[7:]
    qa_ref, ka_ref, va_ref, qb_ref, kb_ref, vb_ref = outs
    d = D_MODEL
    mod = mod_ref[0]
    h = _rms(x_ref[0]) * g_ref[0:1, :] * (1.0 + mod[:, d:2 * d]) + mod[:, 0:d]
    z = _dot(h, w_ref[...])
    o = 0
    qa = z[:, o:o + GQA_Q]; o += GQA_Q
    ka = z[:, o:o + GQA_KV]; o += GQA_KV
    va = z[:, o:o + GQA_KV]; o += GQA_KV
    qb = z[:, o:o + DIFF_QK]; o += DIFF_QK
    kb = z[:, o:o + DIFF_QK]; o += DIFF_QK
    vb = z[:, o:o + DIFF_V]
    ones_bd = ones_ref[...]
    qa = qa * lax.rsqrt(_head_ms(qa, ones_bd) + NORM_EPS) * gq_ref[...]
    ka = ka * lax.rsqrt(_head_ms(ka, ones_bd) + NORM_EPS) * gk_ref[...]
    if rope:
        cos, sa, sb = cos_ref[...], sa_ref[...], sb_ref[...]
        qa, ka, qb, kb = (_rope(t, cos, sa, sb) for t in (qa, ka, qb, kb))
    qa_ref[0] = qa
    ka_ref[0] = ka
    va_ref[0] = va
    qb_ref[0] = qb
    kb_ref[0] = kb
    vb_ref[0] = vb


def _attn_in(x, mod3, row0, rowstride, gains, w_in, gq, gk, ones_bd, rope_tabs, tm=256):
    bx, tx, d = x.shape
    nt = tx // tm
    widths = (GQA_Q, GQA_KV, GQA_KV, DIFF_QK, DIFF_QK, DIFF_V)
    in_specs = [pl.BlockSpec((1, tm, d), lambda b, i: (b, i, 0)),
                _mod_spec(row0, rowstride),
                _full(gains.shape), _full(w_in.shape), _full(gq.shape), _full(gk.shape), _full(ones_bd.shape)]
    args = [x, mod3, gains, w_in, gq, gk, ones_bd]
    if rope_tabs is not None:
        in_specs += [pl.BlockSpec((tm, LANES), lambda b, i: (i, 0))] * 3
        args += list(rope_tabs)
    return pl.pallas_call(
        functools.partial(_attn_in_kernel, rope=rope_tabs is not None),
        grid=(bx, nt),
        in_specs=in_specs,
        out_specs=[pl.BlockSpec((1, tm, w), lambda b, i: (b, i, 0)) for w in widths],
        out_shape=[jax.ShapeDtypeStruct((bx, tx, w), F32) for w in widths],
        compiler_params=_params("parallel", "parallel"),
        name="attn_in",
    )(*args)


def _softmax_parts(s):
    m = jnp.max(s, axis=-1, keepdims=True)
    p = jnp.exp(s - m)
    return p, 1.0 / jnp.sum(p, axis=-1, keepdims=True)


def _attn_kernel(qa_ref, qb_ref, ka_ref, va_ref, kb_ref, vb_ref, lam_ref, sub_ref, o_ref, *, lam_init):
    scale = HEAD_DIM ** -0.5
    lo = lax.broadcasted_iota(jnp.int32, (1, LANES), 1) < HEAD_DIM
    for g in range(GQA_KV_HEADS):
        k = ka_ref[0, :, g * LANES:(g + 1) * LANES]
        v = va_ref[0, :, g * LANES:(g + 1) * LANES]
        for j in range(GQA_HEADS // GQA_KV_HEADS // 2):
            c0 = (g * (GQA_HEADS // GQA_KV_HEADS // 2) + j) * LANES
            q = qa_ref[0, :, c0:c0 + LANES] * scale
            halves = []
            for keep in (lo, jnp.logical_not(lo)):
                qm = jnp.where(keep, q, 0.0)
                p, inv = _softmax_parts(_dot_nt(qm, k))
                halves.append(_dot(p, v) * inv)
            o_ref[0, :, c0:c0 + LANES] = jnp.where(lo, halves[0], halves[1])
    lam4 = lam_ref[...]
    lam = (jnp.exp(jnp.sum(lam4[0:1] * lam4[1:2], axis=-1, keepdims=True))
           - jnp.exp(jnp.sum(lam4[2:3] * lam4[3:4], axis=-1, keepdims=True)) + lam_init)
    for hd in range(DIFF_HEADS):
        c0 = hd * LANES
        q = qb_ref[0, :, c0:c0 + LANES] * scale
        k = kb_ref[0, :, c0:c0 + LANES]
        v = vb_ref[0, :, c0:c0 + LANES]
        p0, inv0 = _softmax_parts(_dot_nt(jnp.where(lo, q, 0.0), k))
        p1, inv1 = _softmax_parts(_dot_nt(jnp.where(lo, 0.0, q), k))
        o = _dot(p0 * inv0 - lam * (p1 * inv1), v)
        o_ref[0, :, GQA_Q + c0:GQA_Q + c0 + LANES] = _rms(o) * sub_ref[...] * (1.0 - lam_init)


def _attention(qa, qb, ka2, va2, kb, vb, lam4, subln, lam_init, tq=256):
    bx, tx, _ = qa.shape
    s = ka2.shape[1]
    kv_spec = lambda w: pl.BlockSpec((1, s, w), lambda b, i: (b, 0, 0))
    q_spec = pl.BlockSpec((1, tq, GQA_Q), lambda b, i: (b, i, 0))
    return pl.pallas_call(
        functools.partial(_attn_kernel, lam_init=lam_init),
        grid=(bx, tx // tq),
        in_specs=[q_spec, q_spec, kv_spec(2 * GQA_KV), kv_spec(2 * GQA_KV), kv_spec(DIFF_QK), kv_spec(DIFF_V),
                  _full(lam4.shape), _full(subln.shape)],
        out_specs=pl.BlockSpec((1, tq, D_MODEL), lambda b, i: (b, i, 0)),
        out_shape=jax.ShapeDtypeStruct((bx, tx, D_MODEL), F32),
        compiler_params=_params("parallel", "parallel"),
        name="attention",
    )(qa, qb, ka2, va2, kb, vb, lam4, subln)


def _out_proj_kernel(m_ref, x_ref, mod_ref, g_ref, w_ref, o_ref):
    d = D_MODEL
    gate = mod_ref[0][:, 2 * d:3 * d]
    o_ref[0] = x_ref[0] + gate * (_rms(_dot(m_ref[0], w_ref[...])) * g_ref[1:2, :])


def _out_proj(mix, x, mod3, row0, rowstride, gains, w, tm=256):
    bx, tx, d = x.shape
    tok = pl.BlockSpec((1, tm, d), lambda b, i: (b, i, 0))
    return pl.pallas_call(
        _out_proj_kernel,
        grid=(bx, tx // tm),
        in_specs=[tok, tok, _mod_spec(row0, rowstride), _full(gains.shape), _full(w.shape)],
        out_specs=tok,
        out_shape=jax.ShapeDtypeStruct(x.shape, F32),
        compiler_params=_params("parallel", "parallel"),
        name="out_proj",
    )(mix, x, mod3, gains, w)


def _mlp_kernel(x_ref, mod_ref, g_ref, w1_ref, w2_ref, o_ref, *, fc):
    d = D_MODEL
    x = x_ref[0]
    mod = mod_ref[0]
    h = (_rms(x) * g_ref[2:3, :] * (1.0 + mod[:, 4 * d:5 * d]) + mod[:, 3 * d:4 * d]).astype(BF16)
    acc = jnp.zeros(x.shape, F32)
    for c in range(D_FF // fc):
        u = jnp.maximum(jnp.dot(h, w1_ref[:, c * fc:(c + 1) * fc], preferred_element_type=F32), 0.0)
        acc = acc + _dot(u * u, w2_ref[c * fc:(c + 1) * fc, :])
    o_ref[0] = x + mod[:, 5 * d:6 * d] * (_rms(acc) * g_ref[3:4, :])


def _mlp(x, mod3, row0, rowstride, gains, w1, w2, tm=512, fc=512):
    bx, tx, d = x.shape
    tok = pl.BlockSpec((1, tm, d), lambda b, i: (b, i, 0))
    return pl.pallas_call(
        functools.partial(_mlp_kernel, fc=fc),
        grid=(bx, tx // tm),
        in_specs=[tok, _mod_spec(row0, rowstride), _full(gains.shape), _full(w1.shape), _full(w2.shape)],
        out_specs=tok,
        out_shape=jax.ShapeDtypeStruct(x.shape, F32),
        compiler_params=_params("parallel", "parallel"),
        name="mlp",
    )(x, mod3, gains, w1, w2)


def _rwkv_in_kernel(x_ref, xp_ref, xn_ref, mod_ref, g_ref, mu_ref, wrkv_ref, wlo_ref, w2_ref, a2_ref, g2_ref,
                    w0_ref, a0_ref, kv_ref, ones_ref,
                    r_ref, v_ref, kk_ref, gate_ref, bonus_ref, lw_ref, kd_ref, ka_ref, *, seq_len):
    d = D_MODEL
    tm = x_ref.shape[1]
    mod = mod_ref[0]
    g0 = g_ref[0:1, :]
    sc = 1.0 + mod[:, d:2 * d]
    sh = mod[:, 0:d]
    norm = lambda t: _rms(t) * g0 * sc + sh
    h = norm(x_ref[0])
    hp = norm(xp_ref[0])[7:8, :]
    hn = norm(xn_ref[0])[0:1, :]
    row = lax.broadcasted_iota(jnp.int32, (tm, 1), 0)
    t_seq = (pl.program_id(1) * tm + row) % seq_len
    prev = jnp.where(row == 0, hp, pltpu.roll(h, 1, 0))
    prev = jnp.where(t_seq == 0, 0.0, prev)
    nxt = jnp.where(row == tm - 1, hn, pltpu.roll(h, tm - 1, 0))
    nxt = jnp.where(t_seq == seq_len - 1, 0.0, nxt)
    xx = 0.5 * (prev + nxt) - h
    xr, xw, xk, xv, xa, xg = (h + xx * mu_ref[n:n + 1, :] for n in range(6))
    r = _dot(xr, wrkv_ref[0])
    k = _dot(xk, wrkv_ref[1])
    v = _dot(xv, wrkv_ref[2])
    gate = _dot(_sigmoid(_dot(xg, wlo_ref[2])), g2_ref[...])
    wl = _dot(jnp.tanh(_dot(xw, wlo_ref[0])), w2_ref[...])
    al = _dot(_dot(xa, wlo_ref[1]), a2_ref[...])
    ones_bd = ones_ref[...]
    k_k, k_a, r_k = kv_ref[0:1, :], kv_ref[1:2, :], kv_ref[2:3, :]
    kk = k * k_k
    kk = kk * lax.rsqrt(_head_ms_wide(kk, ones_bd) * HEAD_DIM + 1e-12)
    bonus = jnp.zeros_like(v)
    for dr in range(2):
        z = -(w0_ref[dr:dr + 1, :] + wl[:, dr * d:(dr + 1) * d])
        softplus = jnp.maximum(z, 0.0) + jnp.log(1.0 + jnp.exp(-jnp.abs(z)))
        lw = -jnp.exp(-softplus - 0.5)
        a = _sigmoid(a0_ref[dr:dr + 1, :] + al[:, dr * d:(dr + 1) * d])
        kd = k * (1.0 + (a - 1.0) * k_a)
        ka = kk * a
        prod = r * kd * r_k
        hi = prod.astype(BF16)
        bsum = jnp.concatenate(
            [jnp.dot(hi[:, j:j + 2 * LANES], ones_bd[0:2 * LANES, 0:2 * LANES], preferred_element_type=F32)
             + _dot(prod[:, j:j + 2 * LANES] - hi[:, j:j + 2 * LANES].astype(F32), ones_bd[0:2 * LANES, 0:2 * LANES])
             for j in range(0, d, 2 * LANES)], axis=-1)
        bonus = bonus + bsum * v
        for p in range(N_PAIRS):
            sl = slice(p * LANES, (p + 1) * LANES)
            lw_ref[dr, 0, p] = lw[:, sl]
            kd_ref[dr, 0, p] = kd[:, sl]
            ka_ref[dr, 0, p] = ka[:, sl]
    for p in range(N_PAIRS):
        sl = slice(p * LANES, (p + 1) * LANES)
        r_ref[0, p] = r[:, sl]
        v_ref[0, p] = v[:, sl]
        kk_ref[0, p] = kk[:, sl]
    gate_ref[0] = gate
    bonus_ref[0] = bonus


def _head_ms_wide(x, ones_bd):
    w = 2 * LANES
    return jnp.concatenate([_dot(x[:, j:j + w] * x[:, j:j + w], ones_bd[0:w, 0:w]) for j in range(0, x.shape[-1], w)],
                           axis=-1) * (1.0 / HEAD_DIM)


def _rwkv_in(x, mod3, row0, rowstride, gains, mu, wrkv, wlo, w2bd, a2bd, g2, w0, a0, kvec, ones_bd, seq_len, tm=256):
    bx, tx, d = x.shape
    nt = tx // tm
    r8 = tm // 8
    pair = jax.ShapeDtypeStruct((bx, N_PAIRS, tx, LANES), F32)
    pair2 = jax.ShapeDtypeStruct((2, bx, N_PAIRS, tx, LANES), F32)
    tok = jax.ShapeDtypeStruct((bx, tx, d), F32)
    pair_spec = pl.BlockSpec((1, N_PAIRS, tm, LANES), lambda b, i: (b, 0, i, 0))
    pair2_spec = pl.BlockSpec((2, 1, N_PAIRS, tm, LANES), lambda b, i: (0, b, 0, i, 0))
    tok_spec = pl.BlockSpec((1, tm, d), lambda b, i: (b, i, 0))
    consts = (gains, mu, wrkv, wlo, w2bd, a2bd, g2, w0, a0, kvec, ones_bd)
    return pl.pallas_call(
        functools.partial(_rwkv_in_kernel, seq_len=seq_len),
        grid=(bx, nt),
        in_specs=[tok_spec,
                  pl.BlockSpec((1, 8, d), lambda b, i: (b, jnp.maximum(i * r8 - 1, 0), 0)),
                  pl.BlockSpec((1, 8, d), lambda b, i: (b, jnp.minimum((i + 1) * r8, tx // 8 - 1), 0)),
                  _mod_spec(row0, rowstride)] + [_full(c.shape) for c in consts],
        out_specs=[pair_spec, pair_spec, pair_spec, tok_spec, tok_spec, pair2_spec, pair2_spec, pair2_spec],
        out_shape=[pair, pair, pair, tok, tok, pair2, pair2, pair2],
        compiler_params=_params("parallel", "parallel"),
        name="rwkv_in",
    )(x, x, x, mod3, *consts)


def _wkv_masks():
    n2 = 2 * CHUNK
    idx = np.arange(n2)
    order64, half64, sq = [], [], []
    for rev in (False, True):
        pos = (n2 - 1 - idx) if rev else idx
        pt, ps = pos[:, None], pos[None, :]
        same = (pt // CHUNK) == (ps // CHUNK)
        mats = [same & (ps < pt), same & (ps <= pt)]
        n = 1
        while n < CHUNK:
            mats.append((pt // (2 * n) == ps // (2 * n)) & ((pt // n) % 2 == 1) & ((ps // n) % 2 == 0))
            n *= 2
        sq.append(np.stack(mats))
        p64 = pos[:CHUNK] % CHUNK if not rev else (CHUNK - 1 - np.arange(CHUNK))
        order64.append(p64[None, :] <= p64[:, None])
        half64.append((p64 < CHUNK // 2)[:, None])
    as32 = lambda m: np.stack(m).astype(np.float32)
    return as32(order64), as32(half64), as32(sq)


def _wkv_unit(r, v, kk, lw, kd, ka, s_bd, order, half, sq_ref, dr):
    lo = lax.broadcasted_iota(jnp.int32, (1, LANES), 1) < HEAD_DIM
    stack = lambda t: jnp.concatenate([jnp.where(lo, t, 0.0), jnp.where(lo, 0.0, t)], axis=0)
    lw_hi = lw.astype(BF16)
    cum = (jnp.dot(order.astype(BF16), lw_hi, preferred_element_type=F32)
           + _dot(order, lw - lw_hi.astype(F32)))
    tot = jnp.sum(lw, axis=0, keepdims=True)
    mid = jnp.sum(lw * half, axis=0, keepdims=True)
    g_inv = jnp.exp(mid - cum)
    g_tail = jnp.exp(tot - cum)
    kk2 = stack(kk * jnp.exp(cum - lw - mid))
    r2 = stack(r * jnp.exp(cum - mid))
    ai = ka * g_inv
    ki = kd * g_inv
    gram = _dot_nt(jnp.concatenate([kk2, r2], axis=0), jnp.concatenate([ai, ai, ki, ki], axis=0))
    n2 = 2 * CHUNK
    before, upto = sq_ref[dr, 0], sq_ref[dr, 1]
    a = gram[0:n2, 0:n2] * before
    b = gram[0:n2, n2:2 * n2] * before
    c = gram[n2:2 * n2, 0:n2] * upto
    dd = gram[n2:2 * n2, n2:2 * n2] * upto
    eye = (lax.broadcasted_iota(jnp.int32, (n2, n2), 0) == lax.broadcasted_iota(jnp.int32, (n2, n2), 1)).astype(F32)
    tinv = eye - a * sq_ref[dr, 2]
    lvl = 3
    n = 2
    while n < CHUNK:
        tinv = tinv - _dot(tinv, _dot(a * sq_ref[dr, lvl], tinv))
        lvl += 1
        n *= 2
    s_mid = s_bd * jnp.exp(mid)
    v2 = stack(v)
    u2 = _dot(tinv, _dot_nt(kk2, s_mid) + _dot(b, v2))
    y2 = _dot_nt(r2, s_mid) + _dot(dd, v2) - _dot(c, u2)
    s_new = s_bd * jnp.exp(tot) + _dot_tn(v2, stack(kd * g_tail)) - _dot_tn(u2, stack(ka * g_tail))
    return y2[0:CHUNK] + y2[CHUNK:n2], s_new


def _wkv_kernel(*refs, has_init, want_final):
    (rf_ref, vf_ref, kkf_ref, rb_ref, vb_ref, kkb_ref, lwf_ref, kdf_ref, kaf_ref, lwb_ref, kdb_ref, kab_ref,
     order_ref, half_ref, sq_ref) = refs[:15]
    rest = list(refs[15:])
    s0_ref = rest.pop(0) if has_init else None
    yf_ref = rest.pop(0)
    yb_ref = rest.pop(0)
    sf_ref = rest.pop(0) if want_final else None
    s_scr = rest.pop(0)
    c = pl.program_id(1)
    nc = pl.num_programs(1)

    @pl.when(c == 0)
    def _():
        if has_init:
            s_scr[...] = s0_ref[0]
        else:
            s_scr[...] = jnp.zeros(s_scr.shape, F32)

    def body(p, carry):
        y, s_new = _wkv_unit(rf_ref[0, p], vf_ref[0, p], kkf_ref[0, p], lwf_ref[0, 0, p], kdf_ref[0, 0, p],
                             kaf_ref[0, 0, p], s_scr[0, p], order_ref[0], half_ref[0], sq_ref, 0)
        yf_ref[0, p] = y
        s_scr[0, p] = s_new
        y, s_new = _wkv_unit(rb_ref[0, p], vb_ref[0, p], kkb_ref[0, p], lwb_ref[0, 0, p], kdb_ref[0, 0, p],
                             kab_ref[0, 0, p], s_scr[1, p], order_ref[1], half_ref[1], sq_ref, 1)
        yb_ref[0, p] = y
        s_scr[1, p] = s_new
        return carry

    lax.fori_loop(0, N_PAIRS, body, 0)

    if want_final:
        @pl.when(c == nc - 1)
        def _():
            for dr in range(2):
                for p in range(N_PAIRS):
                    s = s_scr[dr, p]
                    sf_ref[0, dr, 2 * p] = s[0:HEAD_DIM, 0:HEAD_DIM]
                    sf_ref[0, dr, 2 * p + 1] = s[HEAD_DIM:, HEAD_DIM:]


def _wkv(r, v, kk, lw, kd, ka, s0_bd, want_final):
    bx, _, tx, _ = r.shape
    nc = tx // CHUNK
    order64, half64, sq = (jnp.asarray(m) for m in _wkv_masks())
    fwd = pl.BlockSpec((1, N_PAIRS, CHUNK, LANES), lambda b, c: (b, 0, c, 0))
    bwd = pl.BlockSpec((1, N_PAIRS, CHUNK, LANES), lambda b, c: (b, 0, nc - 1 - c, 0))
    fwd2 = pl.BlockSpec((1, 1, N_PAIRS, CHUNK, LANES), lambda b, c: (0, b, 0, c, 0))
    bwd2 = pl.BlockSpec((1, 1, N_PAIRS, CHUNK, LANES), lambda b, c: (1, b, 0, nc - 1 - c, 0))
    in_specs = [fwd, fwd, fwd, bwd, bwd, bwd, fwd2, fwd2, fwd2, bwd2, bwd2, bwd2,
                _full(order64.shape), _full(half64.shape), _full(sq.shape)]
    args = [r, v, kk, r, v, kk, lw, kd, ka, lw, kd, ka, order64, half64, sq]
    if s0_bd is not None:
        in_specs.append(pl.BlockSpec((1, 2, N_PAIRS, LANES, LANES), lambda b, c: (b, 0, 0, 0, 0)))
        args.append(s0_bd)
    y_shape = jax.ShapeDtypeStruct(r.shape, F32)
    out_specs = [fwd, bwd]
    out_shape = [y_shape, y_shape]
    if want_final:
        out_specs.append(pl.BlockSpec((1, 2, RWKV_HEADS, HEAD_DIM, HEAD_DIM), lambda b, c: (b, 0, 0, 0, 0)))
        out_shape.append(jax.ShapeDtypeStruct((bx, 2, RWKV_HEADS, HEAD_DIM, HEAD_DIM), F32))
    return pl.pallas_call(
        functools.partial(_wkv_kernel, has_init=s0_bd is not None, want_final=want_final),
        grid=(bx, nc),
        in_specs=in_specs,
        out_specs=out_specs,
        out_shape=out_shape,
        scratch_shapes=[pltpu.VMEM((2, N_PAIRS, LANES, LANES), F32)],
        compiler_params=_params("parallel", "arbitrary"),
        name="wkv",
    )(*args)


def _rwkv_out_kernel(yf_ref, yb_ref, bonus_ref, gate_ref, x_ref, mod_ref, g_ref, lnx_ref, w_ref, ones_ref, o_ref):
    d = D_MODEL
    w = 2 * LANES
    ones_bd = ones_ref[...]
    y = jnp.concatenate([yf_ref[0, p] + yb_ref[0, p] for p in range(N_PAIRS)], axis=-1)

    def head_mean(t):
        hi = t.astype(BF16)
        return jnp.concatenate(
            [jnp.dot(hi[:, j:j + w], ones_bd[0:w, 0:w], preferred_element_type=F32)
             + _dot(t[:, j:j + w] - hi[:, j:j + w].astype(F32), ones_bd[0:w, 0:w]) for j in range(0, d, w)],
            axis=-1) * (1.0 / HEAD_DIM)

    yc = y - head_mean(y)
    yn = yc * lax.rsqrt(_head_ms_wide(yc, ones_bd) + LNX_EPS)
    z = (yn * lnx_ref[0:1, :] + lnx_ref[1:2, :] + bonus_ref[0]) * gate_ref[0]
    gate1 = mod_ref[0][:, 2 * d:3 * d]
    o_ref[0] = x_ref[0] + gate1 * (_rms(_dot(z, w_ref[...])) * g_ref[1:2, :])


def _rwkv_out(yf, yb, bonus, gate, x, mod3, row0, rowstride, gains, lnx, w_o, ones_bd, tm=256):
    bx, tx, d = x.shape
    tok = pl.BlockSpec((1, tm, d), lambda b, i: (b, i, 0))
    pair = pl.BlockSpec((1, N_PAIRS, tm, LANES), lambda b, i: (b, 0, i, 0))
    return pl.pallas_call(
        _rwkv_out_kernel,
        grid=(bx, tx // tm),
        in_specs=[pair, pair, tok, tok, tok, _mod_spec(row0, rowstride),
                  _full(gains.shape), _full(lnx.shape), _full(w_o.shape), _full(ones_bd.shape)],
        out_specs=tok,
        out_shape=jax.ShapeDtypeStruct(x.shape, F32),
        compiler_params=_params("parallel", "parallel"),
        name="rwkv_out",
    )(yf, yb, bonus, gate, x, mod3, gains, lnx, w_o, ones_bd)


def _rope_tables(t_len):
    rows = t_len // GRID_W
    row = jnp.repeat(jnp.arange(rows), GRID_W).astype(F32)
    col = jnp.tile(jnp.arange(GRID_W), rows).astype(F32)
    inv = 1.0 / (ROPE_THETA ** (jnp.arange(0, AXIS_DIM, 2, dtype=F32) / AXIS_DIM))
    ar = row[:, None] * inv[None, :]
    ac = col[:, None] * inv[None, :]
    ang = jnp.concatenate([ar, ar, ac, ac], axis=-1)
    cos, sin = jnp.cos(ang), jnp.sin(ang)
    first = (jnp.arange(HEAD_DIM) % AXIS_DIM) < AXIS_DIM // 2
    sin_a = jnp.where(first, -sin, 0.0)
    sin_b = jnp.where(first, 0.0, sin)
    rep = LANES // HEAD_DIM
    return tuple(jnp.tile(t, (1, rep)) for t in (cos, sin_a, sin_b))


def _dup_heads(t):
    lead = t.shape[:-1]
    t = t.reshape(lead + (GQA_KV_HEADS, 1, HEAD_DIM))
    return jnp.broadcast_to(t, lead + (GQA_KV_HEADS, LANES // HEAD_DIM, HEAD_DIM)).reshape(lead + (GQA_KV_HEADS * LANES,))


def _block_diag_pairs(s):
    b = s.shape[0]
    s = s.reshape(b, 2, N_PAIRS, 2, HEAD_DIM, HEAD_DIM)
    z = jnp.zeros_like(s[:, :, :, 0])
    top = jnp.concatenate([s[:, :, :, 0], z], axis=-1)
    bot = jnp.concatenate([z, s[:, :, :, 1]], axis=-1)
    return jnp.concatenate([top, bot], axis=-2)


def kernel(x_prompt, x_sample, c, cache_k_gqa, cache_v_gqa, cache_k_diff, cache_v_diff, state_rwkv, c_ctx, w_ada, b_ada, norm_gains, attn_w_in, attn_w_out, attn_qk_gain, diff_lambda, diff_subln, rwkv_mu, rwkv_w_rkv, rwkv_w_o, rwkv_w0, rwkv_w1, rwkv_w2, rwkv_a0, rwkv_a1, rwkv_a2, rwkv_g1, rwkv_g2, rwkv_kvec, rwkv_lnx, mlp_w1, mlp_w2):
    d = D_MODEL
    nb, seq, _ = x_prompt.shape
    nd, dseq, _ = x_sample.shape
    cond = jnp.concatenate([c_ctx[None, :], c, jnp.zeros((COND_ROWS - 1 - nd, d), F32)], axis=0)
    mod_all = _adaln(cond, w_ada, b_ada)
    hd_ones = np.kron(np.eye(GQA_Q // HEAD_DIM, dtype=np.float32), np.ones((HEAD_DIM, HEAD_DIM), np.float32))
    ones_bd = jnp.asarray(hd_ones, BF16)
    rope_tabs = _rope_tables(dseq)
    xp, xs = x_prompt, x_sample
    streams = lambda: ((xp, 0, 0), (xs, 1, 1))
    kg, vg, kdf, vdf, st = [], [], [], [], []
    for i in range(DEPTH):
        j = i // 2
        mod3 = mod_all[i].reshape(COND_ROWS, 1, 6 * d)
        gains = norm_gains[i]
        if i % 2 == 0:
            lam_init = 0.8 - 0.6 * math.exp(-0.3 * i)
            w_in = attn_w_in[j].astype(BF16)
            w_out = attn_w_out[j].astype(BF16)
            gq = jnp.tile(attn_qk_gain[j, 0], GQA_HEADS)[None, :]
            gk = jnp.tile(attn_qk_gain[j, 1], GQA_KV_HEADS)[None, :]
            subln = diff_subln[j][None, :]
            qa, ka, va, qb, kb, vb = _attn_in(xp, mod3, 0, 0, gains, w_in, gq, gk, ones_bd, None)
            kg.append(ka.reshape(nb, seq, GQA_KV_HEADS, HEAD_DIM))
            vg.append(va.reshape(nb, seq, GQA_KV_HEADS, HEAD_DIM))
            kdf.append(kb.reshape(nb, seq, DIFF_HEADS, 2, HEAD_DIM))
            vdf.append(vb.reshape(nb, seq, DIFF_HEADS, 2 * HEAD_DIM))
            mix = _attention(qa, qb, _dup_heads(ka).astype(BF16), _dup_heads(va).astype(BF16),
                             kb.astype(BF16), vb.astype(BF16), diff_lambda[j], subln, lam_init)
            xp = _out_proj(mix, xp, mod3, 0, 0, gains, w_out)
            qa, ka, va, qb, kb, vb = _attn_in(xs, mod3, 1, 1, gains, w_in, gq, gk, ones_bd, rope_tabs)
            past = cache_k_gqa.shape[2]
            cat = lambda cached, new: jnp.concatenate([cached.astype(BF16), new.astype(BF16)], axis=1)
            ka2 = cat(_dup_heads(cache_k_gqa[:, j].reshape(nd, past, GQA_KV)), _dup_heads(ka))
            va2 = cat(_dup_heads(cache_v_gqa[:, j].reshape(nd, past, GQA_KV)), _dup_heads(va))
            kb2 = cat(cache_k_diff[:, j].reshape(nd, past, DIFF_QK), kb)
            vb2 = cat(cache_v_diff[:, j].reshape(nd, past, DIFF_V), vb)
            mix = _attention(qa, qb, ka2, va2, kb2, vb2, diff_lambda[j], subln, lam_init)
            xs = _out_proj(mix, xs, mod3, 1, 1, gains, w_out)
        else:
            wrkv = rwkv_w_rkv[j].astype(BF16)
            wlo = jnp.stack([jnp.concatenate([rwkv_w1[j, 0], rwkv_w1[j, 1]], axis=-1),
                             jnp.concatenate([rwkv_a1[j, 0], rwkv_a1[j, 1]], axis=-1),
                             rwkv_g1[j]]).astype(BF16)
            zlo = jnp.zeros_like(rwkv_w2[j, 0])
            w2bd = jnp.concatenate([jnp.concatenate([rwkv_w2[j, 0], zlo], axis=-1),
                                    jnp.concatenate([zlo, rwkv_w2[j, 1]], axis=-1)], axis=0).astype(BF16)
            a2bd = jnp.concatenate([jnp.concatenate([rwkv_a2[j, 0], zlo], axis=-1),
                                    jnp.concatenate([zlo, rwkv_a2[j, 1]], axis=-1)], axis=0).astype(BF16)
            g2 = rwkv_g2[j].astype(BF16)
            w_o = rwkv_w_o[j].astype(BF16)
            new = []
            for x, row0, rowstride in streams():
                sample = row0 == 1
                r, v, kk, gate, bonus, lw, kd, ka = _rwkv_in(
                    x, mod3, row0, rowstride, gains, rwkv_mu[j], wrkv, wlo, w2bd, a2bd, g2,
                    rwkv_w0[j], rwkv_a0[j], rwkv_kvec[j], ones_bd, x.shape[1])
                s0 = _block_diag_pairs(state_rwkv[:, j]) if sample else None
                res = _wkv(r, v, kk, lw, kd, ka, s0, want_final=not sample)
                if not sample:
                    st.append(res[2])
                new.append(_rwkv_out(res[0], res[1], bonus, gate, x, mod3, row0, rowstride, gains,
                                     rwkv_lnx[j], w_o, ones_bd))
            xp, xs = new
        w1 = mlp_w1[i].astype(BF16)
        w2 = mlp_w2[i].astype(BF16)
        xp = _mlp(xp.reshape(nb * seq // 512, 512, d), mod3, 0, 0, gains, w1, w2).reshape(nb, seq, d)
        xs = _mlp(xs, mod3, 1, 1, gains, w1, w2)
    return (xp, xs, jnp.stack(kg, axis=1), jnp.stack(vg, axis=1), jnp.stack(kdf, axis=1), jnp.stack(vdf, axis=1),
            jnp.stack(st, axis=1))
```

```python
import functools
import math

import numpy as np
import jax
import jax.numpy as jnp
from jax import lax
from jax.experimental import pallas as pl
from jax.experimental.pallas import tpu as pltpu

F32 = jnp.float32
BF16 = jnp.bfloat16

D_MODEL = 1024
DEPTH = 4
GRID_W = 64
HEAD_DIM = 64
AXIS_DIM = HEAD_DIM // 2
ROPE_THETA = 10000.0
GQA_HEADS = 8
GQA_KV_HEADS = 2
DIFF_HEADS = 4
RWKV_HEADS = D_MODEL // HEAD_DIM
D_FF = 4 * D_MODEL
GQA_Q = GQA_HEADS * HEAD_DIM
GQA_KV = GQA_KV_HEADS * HEAD_DIM
DIFF_QK = DIFF_HEADS * 2 * HEAD_DIM
DIFF_V = DIFF_HEADS * 2 * HEAD_DIM
ATTN_IN = GQA_Q + 2 * GQA_KV + 2 * DIFF_QK + DIFF_V
NORM_EPS = 1e-6
LNX_EPS = 64e-5

LANES = 128
N_PAIRS = D_MODEL // LANES
CHUNK = 64
COND_ROWS = 8
VMEM_LIMIT = 56 * 1024 * 1024


def _dot(a, b):
    return jnp.dot(a.astype(BF16), b.astype(BF16), preferred_element_type=F32)


def _dot_nt(a, b):
    return lax.dot_general(a.astype(BF16), b.astype(BF16), (((1,), (1,)), ((), ())), preferred_element_type=F32)


def _dot_tn(a, b):
    return lax.dot_general(a.astype(BF16), b.astype(BF16), (((0,), (0,)), ((), ())), preferred_element_type=F32)


def _rms(x):
    return x * lax.rsqrt(jnp.mean(x * x, axis=-1, keepdims=True) + NORM_EPS)


def _sigmoid(x):
    return 1.0 / (1.0 + jnp.exp(-x))


def _params(*sem):
    return pltpu.CompilerParams(dimension_semantics=sem, vmem_limit_bytes=VMEM_LIMIT)


def _full(shape):
    nd = len(shape)
    return pl.BlockSpec(shape, lambda *_: (0,) * nd)


def _mod_spec(row0, rowstride):
    return pl.BlockSpec((1, 1, 6 * D_MODEL), lambda b, i: (row0 + b * rowstride, 0, 0))


def _adaln_kernel(c_ref, w_ref, b_ref, o_ref):
    c = c_ref[...]
    o_ref[0] = _dot(c * _sigmoid(c), w_ref[0]) + b_ref[0]


def _adaln(cond, w_ada, b_ada):
    tn = 1536
    return pl.pallas_call(
        _adaln_kernel,
        grid=(DEPTH, 6 * D_MODEL // tn),
        in_specs=[_full((COND_ROWS, D_MODEL)),
                  pl.BlockSpec((1, D_MODEL, tn), lambda l, j: (l, 0, j)),
                  pl.BlockSpec((1, 1, tn), lambda l, j: (l, 0, j))],
        out_specs=pl.BlockSpec((1, COND_ROWS, tn), lambda l, j: (l, 0, j)),
        out_shape=jax.ShapeDtypeStruct((DEPTH, COND_ROWS, 6 * D_MODEL), F32),
        compiler_params=_params("parallel", "parallel"),
        name="adaln",
    )(cond, w_ada, b_ada.reshape(DEPTH, 1, 6 * D_MODEL))


def _head_ms(x, ones_bd):
    w = x.shape[-1]
    return _dot(x * x, ones_bd[0:w, 0:w]) * (1.0 / HEAD_DIM)


def _rope(x, cos, sin_a, sin_b):
    cols = []
    for j in range(x.shape[-1] // LANES):
        xj = x[:, j * LANES:(j + 1) * LANES]
        cols.append(xj * cos + pltpu.roll(xj, LANES - AXIS_DIM // 2, 1) * sin_a + pltpu.roll(xj, AXIS_DIM // 2, 1) * sin_b)
    return cols[0] if len(cols) == 1 else jnp.concatenate(cols, axis=-1)


def _attn_in_kernel(*refs, rope):
    if rope:
        x_ref, mod_ref, g_ref, w_ref, gq_ref, gk_ref, ones_ref, cos_ref, sa_ref, sb_ref = refs[:10]
        outs = refs[10:]
    else:
        x_ref, mod_ref, g_ref, w_ref, gq_ref, gk_ref, ones_ref = refs[:7]
        outs = refs[7:]
    qa_ref, ka_ref, va_ref, qb_ref, kb_ref, vb_ref = outs
    d = D_MODEL
    mod = mod_ref[0]
    h = _rms(x_ref[0]) * g_ref[0:1, :] * (1.0 + mod[:, d:2 * d]) + mod[:, 0:d]
    z = _dot(h, w_ref[...])
    o = 0
    qa = z[:, o:o + GQA_Q]; o += GQA_Q
    ka = z[:, o:o + GQA_KV]; o += GQA_KV
    va = z[:, o:o + GQA_KV]; o += GQA_KV
    qb = z[:, o:o + DIFF_QK]; o += DIFF_QK
    kb = z[:, o:o + DIFF_QK]; o += DIFF_QK
    vb = z[:, o:o + DIFF_V]
    ones_bd = ones_ref[...]
    qa = qa * lax.rsqrt(_head_ms(qa, ones_bd) + NORM_EPS) * gq_ref[...]
    ka = ka * lax.rsqrt(_head_ms(ka, ones_bd) + NORM_EPS) * gk_ref[...]
    if rope:
        cos, sa, sb = cos_ref[...], sa_ref[...], sb_ref[...]
        qa, ka, qb, kb = (_rope(t, cos, sa, sb) for t in (qa, ka, qb, kb))
    qa_ref[0] = qa
    ka_ref[0] = ka
    va_ref[0] = va
    qb_ref[0] = qb
    kb_ref[0] = kb
    vb_ref[0] = vb


def _attn_in(x, mod3, row0, rowstride, gains, w_in, gq, gk, ones_bd, rope_tabs, tm=256):
    bx, tx, d = x.shape
    nt = tx // tm
    widths = (GQA_Q, GQA_KV, GQA_KV, DIFF_QK, DIFF_QK, DIFF_V)
    in_specs = [pl.BlockSpec((1, tm, d), lambda b, i: (b, i, 0)),
                _mod_spec(row0, rowstride),
                _full(gains.shape), _full(w_in.shape), _full(gq.shape), _full(gk.shape), _full(ones_bd.shape)]
    args = [x, mod3, gains, w_in, gq, gk, ones_bd]
    if rope_tabs is not None:
        in_specs += [pl.BlockSpec((tm, LANES), lambda b, i: (i, 0))] * 3
        args += list(rope_tabs)
    return pl.pallas_call(
        functools.partial(_attn_in_kernel, rope=rope_tabs is not None),
        grid=(bx, nt),
        in_specs=in_specs,
        out_specs=[pl.BlockSpec((1, tm, w), lambda b, i: (b, i, 0)) for w in widths],
        out_shape=[jax.ShapeDtypeStruct((bx, tx, w), F32) for w in widths],
        compiler_params=_params("parallel", "parallel"),
        name="attn_in",
    )(*args)


def _softmax_parts(s):
    m = jnp.max(s, axis=-1, keepdims=True)
    p = jnp.exp(s - m)
    return p, 1.0 / jnp.sum(p, axis=-1, keepdims=True)


def _attn_kernel(qa_ref, qb_ref, ka_ref, va_ref, kb_ref, vb_ref, lam_ref, sub_ref, o_ref, *, lam_init):
    scale = HEAD_DIM ** -0.5
    lo = lax.broadcasted_iota(jnp.int32, (1, LANES), 1) < HEAD_DIM
    for g in range(GQA_KV_HEADS):
        k = ka_ref[0, :, g * LANES:(g + 1) * LANES]
        v = va_ref[0, :, g * LANES:(g + 1) * LANES]
        for j in range(GQA_HEADS // GQA_KV_HEADS // 2):
            c0 = (g * (GQA_HEADS // GQA_KV_HEADS // 2) + j) * LANES
            q = qa_ref[0, :, c0:c0 + LANES] * scale
            halves = []
            for keep in (lo, jnp.logical_not(lo)):
                qm = jnp.where(keep, q, 0.0)
                p, inv = _softmax_parts(_dot_nt(qm, k))
                halves.append(_dot(p, v) * inv)
            o_ref[0, :, c0:c0 + LANES] = jnp.where(lo, halves[0], halves[1])
    lam4 = lam_ref[...]
    lam = (jnp.exp(jnp.sum(lam4[0:1] * lam4[1:2], axis=-1, keepdims=True))
           - jnp.exp(jnp.sum(lam4[2:3] * lam4[3:4], axis=-1, keepdims=True)) + lam_init)
    for hd in range(DIFF_HEADS):
        c0 = hd * LANES
        q = qb_ref[0, :, c0:c0 + LANES] * scale
        k = kb_ref[0, :, c0:c0 + LANES]
        v = vb_ref[0, :, c0:c0 + LANES]
        p0, inv0 = _softmax_parts(_dot_nt(jnp.where(lo, q, 0.0), k))
        p1, inv1 = _softmax_parts(_dot_nt(jnp.where(lo, 0.0, q), k))
        o = _dot(p0 * inv0 - lam * (p1 * inv1), v)
        o_ref[0, :, GQA_Q + c0:GQA_Q + c0 + LANES] = _rms(o) * sub_ref[...] * (1.0 - lam_init)


def _attention(qa, qb, ka2, va2, kb, vb, lam4, subln, lam_init, tq=256):
    bx, tx, _ = qa.shape
    s = ka2.shape[1]
    kv_spec = lambda w: pl.BlockSpec((1, s, w), lambda b, i: (b, 0, 0))
    q_spec = pl.BlockSpec((1, tq, GQA_Q), lambda b, i: (b, i, 0))
    return pl.pallas_call(
        functools.partial(_attn_kernel, lam_init=lam_init),
        grid=(bx, tx // tq),
        in_specs=[q_spec, q_spec, kv_spec(2 * GQA_KV), kv_spec(2 * GQA_KV), kv_spec(DIFF_QK), kv_spec(DIFF_V),
                  _full(lam4.shape), _full(subln.shape)],
        out_specs=pl.BlockSpec((1, tq, D_MODEL), lambda b, i: (b, i, 0)),
        out_shape=jax.ShapeDtypeStruct((bx, tx, D_MODEL), F32),
        compiler_params=_params("parallel", "parallel"),
        name="attention",
    )(qa, qb, ka2, va2, kb, vb, lam4, subln)


def _out_proj_kernel(m_ref, x_ref, mod_ref, g_ref, w_ref, o_ref):
    d = D_MODEL
    gate = mod_ref[0][:, 2 * d:3 * d]
    o_ref[0] = x_ref[0] + gate * (_rms(_dot(m_ref[0], w_ref[...])) * g_ref[1:2, :])


def _out_proj(mix, x, mod3, row0, rowstride, gains, w, tm=256):
    bx, tx, d = x.shape
    tok = pl.BlockSpec((1, tm, d), lambda b, i: (b, i, 0))
    return pl.pallas_call(
        _out_proj_kernel,
        grid=(bx, tx // tm),
        in_specs=[tok, tok, _mod_spec(row0, rowstride), _full(gains.shape), _full(w.shape)],
        out_specs=tok,
        out_shape=jax.ShapeDtypeStruct(x.shape, F32),
        compiler_params=_params("parallel", "parallel"),
        name="out_proj",
    )(mix, x, mod3, gains, w)


def _mlp_kernel(x_ref, mod_ref, g_ref, w1_ref, w2_ref, o_ref, *, fc):
    d = D_MODEL
    x = x_ref[0]
    mod = mod_ref[0]
    h = (_rms(x) * g_ref[2:3, :] * (1.0 + mod[:, 4 * d:5 * d]) + mod[:, 3 * d:4 * d]).astype(BF16)
    acc = jnp.zeros(x.shape, F32)
    for c in range(D_FF // fc):
        u = jnp.maximum(jnp.dot(h, w1_ref[:, c * fc:(c + 1) * fc], preferred_element_type=F32), 0.0)
        acc = acc + _dot(u * u, w2_ref[c * fc:(c + 1) * fc, :])
    o_ref[0] = x + mod[:, 5 * d:6 * d] * (_rms(acc) * g_ref[3:4, :])


def _mlp(x, mod3, row0, rowstride, gains, w1, w2, tm=512, fc=512):
    bx, tx, d = x.shape
    tok = pl.BlockSpec((1, tm, d), lambda b, i: (b, i, 0))
    return pl.pallas_call(
        functools.partial(_mlp_kernel, fc=fc),
        grid=(bx, tx // tm),
        in_specs=[tok, _mod_spec(row0, rowstride), _full(gains.shape), _full(w1.shape), _full(w2.shape)],
        out_specs=tok,
        out_shape=jax.ShapeDtypeStruct(x.shape, F32),
        compiler_params=_params("parallel", "parallel"),
        name="mlp",
    )(x, mod3, gains, w1, w2)


def _rwkv_in_kernel(x_ref, xp_ref, xn_ref, mod_ref, g_ref, mu_ref, wrkv_ref, wlo_ref, w2_ref, a2_ref, g2_ref,
                    w0_ref, a0_ref, kv_ref, ones_ref,
                    r_ref, v_ref, kk_ref, gate_ref, bonus_ref, lw_ref, kd_ref, ka_ref, *, seq_len):
    d = D_MODEL
    tm = x_ref.shape[1]
    mod = mod_ref[0]
    g0 = g_ref[0:1, :]
    sc = 1.0 + mod[:, d:2 * d]
    sh = mod[:, 0:d]
    norm = lambda t: _rms(t) * g0 * sc + sh
    h = norm(x_ref[0])
    hp = norm(xp_ref[0])[7:8, :]
    hn = norm(xn_ref[0])[0:1, :]
    row = lax.broadcasted_iota(jnp.int32, (tm, 1), 0)
    t_seq = (pl.program_id(1) * tm + row) % seq_len
    prev = jnp.where(row == 0, hp, pltpu.roll(h, 1, 0))
    prev = jnp.where(t_seq == 0, 0.0, prev)
    nxt = jnp.where(row == tm - 1, hn, pltpu.roll(h, tm - 1, 0))
    nxt = jnp.where(t_seq == seq_len - 1, 0.0, nxt)
    xx = 0.5 * (prev + nxt) - h
    xr, xw, xk, xv, xa, xg = (h + xx * mu_ref[n:n + 1, :] for n in range(6))
    r = _dot(xr, wrkv_ref[0])
    k = _dot(xk, wrkv_ref[1])
    v = _dot(xv, wrkv_ref[2])
    gate = _dot(_sigmoid(_dot(xg, wlo_ref[2])), g2_ref[...])
    wl = _dot(jnp.tanh(_dot(xw, wlo_ref[0])), w2_ref[...])
    al = _dot(_dot(xa, wlo_ref[1]), a2_ref[...])
    ones_bd = ones_ref[...]
    k_k, k_a, r_k = kv_ref[0:1, :], kv_ref[1:2, :], kv_ref[2:3, :]
    kk = k * k_k
    kk = kk * lax.rsqrt(_head_ms_wide(kk, ones_bd) * HEAD_DIM + 1e-12)
    bonus = jnp.zeros_like(v)
    for dr in range(2):
        z = -(w0_ref[dr:dr + 1, :] + wl[:, dr * d:(dr + 1) * d])
        softplus = jnp.maximum(z, 0.0) + jnp.log(1.0 + jnp.exp(-jnp.abs(z)))
        lw = -jnp.exp(-softplus - 0.5)
        a = _sigmoid(a0_ref[dr:dr + 1, :] + al[:, dr * d:(dr + 1) * d])
        kd = k * (1.0 + (a - 1.0) * k_a)
        ka = kk * a
        prod = r * kd * r_k
        hi = prod.astype(BF16)
        bsum = jnp.concatenate(
            [jnp.dot(hi[:, j:j + 2 * LANES], ones_bd[0:2 * LANES, 0:2 * LANES], preferred_element_type=F32)
             + _dot(prod[:, j:j + 2 * LANES] - hi[:, j:j + 2 * LANES].astype(F32), ones_bd[0:2 * LANES, 0:2 * LANES])
             for j in range(0, d, 2 * LANES)], axis=-1)
        bonus = bonus + bsum * v
        for p in range(N_PAIRS):
            sl = slice(p * LANES, (p + 1) * LANES)
            lw_ref[dr, 0, p] = lw[:, sl]
            kd_ref[dr, 0, p] = kd[:, sl]
            ka_ref[dr, 0, p] = ka[:, sl]
    for p in range(N_PAIRS):
        sl = slice(p * LANES, (p + 1) * LANES)
        r_ref[0, p] = r[:, sl]
        v_ref[0, p] = v[:, sl]
        kk_ref[0, p] = kk[:, sl]
    gate_ref[0] = gate
    bonus_ref[0] = bonus


def _head_ms_wide(x, ones_bd):
    w = 2 * LANES
    return jnp.concatenate([_dot(x[:, j:j + w] * x[:, j:j + w], ones_bd[0:w, 0:w]) for j in range(0, x.shape[-1], w)],
                           axis=-1) * (1.0 / HEAD_DIM)


def _rwkv_in(x, mod3, row0, rowstride, gains, mu, wrkv, wlo, w2bd, a2bd, g2, w0, a0, kvec, ones_bd, seq_len, tm=256):
    bx, tx, d = x.shape
    nt = tx // tm
    r8 = tm // 8
    pair = jax.ShapeDtypeStruct((bx, N_PAIRS, tx, LANES), F32)
    pair2 = jax.ShapeDtypeStruct((2, bx, N_PAIRS, tx, LANES), F32)
    tok = jax.ShapeDtypeStruct((bx, tx, d), F32)
    pair_spec = pl.BlockSpec((1, N_PAIRS, tm, LANES), lambda b, i: (b, 0, i, 0))
    pair2_spec = pl.BlockSpec((2, 1, N_PAIRS, tm, LANES), lambda b, i: (0, b, 0, i, 0))
    tok_spec = pl.BlockSpec((1, tm, d), lambda b, i: (b, i, 0))
    consts = (gains, mu, wrkv, wlo, w2bd, a2bd, g2, w0, a0, kvec, ones_bd)
    return pl.pallas_call(
        functools.partial(_rwkv_in_kernel, seq_len=seq_len),
        grid=(bx, nt),
        in_specs=[tok_spec,
                  pl.BlockSpec((1, 8, d), lambda b, i: (b, jnp.maximum(i * r8 - 1, 0), 0)),
                  pl.BlockSpec((1, 8, d), lambda b, i: (b, jnp.minimum((i + 1) * r8, tx // 8 - 1), 0)),
                  _mod_spec(row0, rowstride)] + [_full(c.shape) for c in consts],
        out_specs=[pair_spec, pair_spec, pair_spec, tok_spec, tok_spec, pair2_spec, pair2_spec, pair2_spec],
        out_shape=[pair, pair, pair, tok, tok, pair2, pair2, pair2],
        compiler_params=_params("parallel", "parallel"),
        name="rwkv_in",
    )(x, x, x, mod3, *consts)


def _wkv_masks():
    n2 = 2 * CHUNK
    idx = np.arange(n2)
    order64, half64, sq = [], [], []
    for rev in (False, True):
        pos = (n2 - 1 - idx) if rev else idx
        pt, ps = pos[:, None], pos[None, :]
        same = (pt // CHUNK) == (ps // CHUNK)
        mats = [same & (ps < pt), same & (ps <= pt)]
        n = 1
        while n < CHUNK:
            mats.append((pt // (2 * n) == ps // (2 * n)) & ((pt // n) % 2 == 1) & ((ps // n) % 2 == 0))
            n *= 2
        sq.append(np.stack(mats))
        p64 = pos[:CHUNK] % CHUNK if not rev else (CHUNK - 1 - np.arange(CHUNK))
        order64.append(p64[None, :] <= p64[:, None])
        half64.append((p64 < CHUNK // 2)[:, None])
    as32 = lambda m: np.stack(m).astype(np.float32)
    return as32(order64), as32(half64), as32(sq)


def _bdot(a, b):
    return lax.dot_general(a.astype(BF16), b.astype(BF16), (((2,), (1,)), ((0,), (0,))), preferred_element_type=F32)


def _bdot_nt(a, b):
    return lax.dot_general(a.astype(BF16), b.astype(BF16), (((2,), (2,)), ((0,), (0,))), preferred_element_type=F32)


def _bdot_tn(a, b):
    return jnp.stack([_dot_tn(a[g], b[g]) for g in range(a.shape[0])])


def _wkv_chunk(r, v, kk, lw, kd, ka, s_bd, order, half, sq):
    g = 2 * N_PAIRS
    n2 = 2 * CHUNK

    def per_dir(x, m):
        return (x.reshape((2, N_PAIRS) + x.shape[1:]) * m[:, None]).reshape(x.shape)

    lo = lax.broadcasted_iota(jnp.int32, (1, 1, LANES), 2) < HEAD_DIM
    stack = lambda t: jnp.concatenate([jnp.where(lo, t, 0.0), jnp.where(lo, 0.0, t)], axis=1)
    order_g = jnp.broadcast_to(order[:, None], (2, N_PAIRS, CHUNK, CHUNK)).reshape(g, CHUNK, CHUNK).astype(BF16)
    lw_hi = lw.astype(BF16)
    cum = _bdot(order_g, lw_hi) + _bdot(order_g, lw - lw_hi.astype(F32))
    tot = jnp.sum(lw, axis=1, keepdims=True)
    mid = jnp.sum(per_dir(lw, half), axis=1, keepdims=True)
    g_inv = jnp.exp(mid - cum)
    g_tail = jnp.exp(tot - cum)
    kk2 = stack(kk * jnp.exp(cum - lw - mid))
    r2 = stack(r * jnp.exp(cum - mid))
    ai = ka * g_inv
    ki = kd * g_inv
    gram = _bdot_nt(jnp.concatenate([kk2, r2], axis=1), jnp.concatenate([ai, ai, ki, ki], axis=1))
    a = per_dir(gram[:, 0:n2, 0:n2], sq[:, 0])
    b = per_dir(gram[:, 0:n2, n2:2 * n2], sq[:, 0])
    c = per_dir(gram[:, n2:2 * n2, 0:n2], sq[:, 1])
    dd = per_dir(gram[:, n2:2 * n2, n2:2 * n2], sq[:, 1])
    eye = (lax.broadcasted_iota(jnp.int32, (1, n2, n2), 1) == lax.broadcasted_iota(jnp.int32, (1, n2, n2), 2)).astype(F32)
    tinv = eye - per_dir(a, sq[:, 2])
    lvl = 3
    n = 2
    while n < CHUNK:
        tinv = tinv - _bdot(tinv, _bdot(per_dir(a, sq[:, lvl]), tinv))
        lvl += 1
        n *= 2
    s_mid = s_bd * jnp.exp(mid)
    v2 = stack(v)
    u2 = _bdot(tinv, _bdot_nt(kk2, s_mid) + _bdot(b, v2))
    y2 = _bdot_nt(r2, s_mid) + _bdot(dd, v2) - _bdot(c, u2)
    s_new = s_bd * jnp.exp(tot) + _bdot_tn(v2, stack(kd * g_tail)) - _bdot_tn(u2, stack(ka * g_tail))
    return y2[:, 0:CHUNK] + y2[:, CHUNK:n2], s_new


def _wkv_kernel(*refs, has_init, want_final):
    (rf_ref, vf_ref, kkf_ref, rb_ref, vb_ref, kkb_ref, lwf_ref, kdf_ref, kaf_ref, lwb_ref, kdb_ref, kab_ref,
     order_ref, half_ref, sq_ref) = refs[:15]
    rest = list(refs[15:])
    s0_ref = rest.pop(0) if has_init else None
    yf_ref = rest.pop(0)
    yb_ref = rest.pop(0)
    sf_ref = rest.pop(0) if want_final else None
    s_scr = rest.pop(0)
    c = pl.program_id(1)
    nc = pl.num_programs(1)

    @pl.when(c == 0)
    def _():
        if has_init:
            s_scr[...] = s0_ref[0].reshape(s_scr.shape)
        else:
            s_scr[...] = jnp.zeros(s_scr.shape, F32)

    both = lambda f_ref, b_ref: jnp.concatenate([f_ref[...].reshape(N_PAIRS, CHUNK, LANES),
                                                 b_ref[...].reshape(N_PAIRS, CHUNK, LANES)], axis=0)
    y, s_new = _wkv_chunk(both(rf_ref, rb_ref), both(vf_ref, vb_ref), both(kkf_ref, kkb_ref), both(lwf_ref, lwb_ref),
                          both(kdf_ref, kdb_ref), both(kaf_ref, kab_ref), s_scr[...],
                          order_ref[...], half_ref[...], sq_ref[...])
    yf_ref[0] = y[0:N_PAIRS]
    yb_ref[0] = y[N_PAIRS:]
    s_scr[...] = s_new

    if want_final:
        @pl.when(c == nc - 1)
        def _():
            for dr in range(2):
                for p in range(N_PAIRS):
                    s = s_scr[dr * N_PAIRS + p]
                    sf_ref[0, dr, 2 * p] = s[0:HEAD_DIM, 0:HEAD_DIM]
                    sf_ref[0, dr, 2 * p + 1] = s[HEAD_DIM:, HEAD_DIM:]


def _wkv(r, v, kk, lw, kd, ka, s0_bd, want_final):
    bx, _, tx, _ = r.shape
    nc = tx // CHUNK
    order64, half64, sq = (jnp.asarray(m) for m in _wkv_masks())
    fwd = pl.BlockSpec((1, N_PAIRS, CHUNK, LANES), lambda b, c: (b, 0, c, 0))
    bwd = pl.BlockSpec((1, N_PAIRS, CHUNK, LANES), lambda b, c: (b, 0, nc - 1 - c, 0))
    fwd2 = pl.BlockSpec((1, 1, N_PAIRS, CHUNK, LANES), lambda b, c: (0, b, 0, c, 0))
    bwd2 = pl.BlockSpec((1, 1, N_PAIRS, CHUNK, LANES), lambda b, c: (1, b, 0, nc - 1 - c, 0))
    in_specs = [fwd, fwd, fwd, bwd, bwd, bwd, fwd2, fwd2, fwd2, bwd2, bwd2, bwd2,
                _full(order64.shape), _full(half64.shape), _full(sq.shape)]
    args = [r, v, kk, r, v, kk, lw, kd, ka, lw, kd, ka, order64, half64, sq]
    if s0_bd is not None:
        in_specs.append(pl.BlockSpec((1, 2, N_PAIRS, LANES, LANES), lambda b, c: (b, 0, 0, 0, 0)))
        args.append(s0_bd)
    y_shape = jax.ShapeDtypeStruct(r.shape, F32)
    out_specs = [fwd, bwd]
    out_shape = [y_shape, y_shape]
    if want_final:
        out_specs.append(pl.BlockSpec((1, 2, RWKV_HEADS, HEAD_DIM, HEAD_DIM), lambda b, c: (b, 0, 0, 0, 0)))
        out_shape.append(jax.ShapeDtypeStruct((bx, 2, RWKV_HEADS, HEAD_DIM, HEAD_DIM), F32))
    return pl.pallas_call(
        functools.partial(_wkv_kernel, has_init=s0_bd is not None, want_final=want_final),
        grid=(bx, nc),
        in_specs=in_specs,
        out_specs=out_specs,
        out_shape=out_shape,
        scratch_shapes=[pltpu.VMEM((2 * N_PAIRS, LANES, LANES), F32)],
        compiler_params=_params("parallel", "arbitrary"),
        name="wkv",
    )(*args)


def _rwkv_out_kernel(yf_ref, yb_ref, bonus_ref, gate_ref, x_ref, mod_ref, g_ref, lnx_ref, w_ref, ones_ref, o_ref):
    d = D_MODEL
    w = 2 * LANES
    ones_bd = ones_ref[...]
    y = jnp.concatenate([yf_ref[0, p] + yb_ref[0, p] for p in range(N_PAIRS)], axis=-1)

    def head_mean(t):
        hi = t.astype(BF16)
        return jnp.concatenate(
            [jnp.dot(hi[:, j:j + w], ones_bd[0:w, 0:w], preferred_element_type=F32)
             + _dot(t[:, j:j + w] - hi[:, j:j + w].astype(F32), ones_bd[0:w, 0:w]) for j in range(0, d, w)],
            axis=-1) * (1.0 / HEAD_DIM)

    yc = y - head_mean(y)
    yn = yc * lax.rsqrt(_head_ms_wide(yc, ones_bd) + LNX_EPS)
    z = (yn * lnx_ref[0:1, :] + lnx_ref[1:2, :] + bonus_ref[0]) * gate_ref[0]
    gate1 = mod_ref[0][:, 2 * d:3 * d]
    o_ref[0] = x_ref[0] + gate1 * (_rms(_dot(z, w_ref[...])) * g_ref[1:2, :])


def _rwkv_out(yf, yb, bonus, gate, x, mod3, row0, rowstride, gains, lnx, w_o, ones_bd, tm=256):
    bx, tx, d = x.shape
    tok = pl.BlockSpec((1, tm, d), lambda b, i: (b, i, 0))
    pair = pl.BlockSpec((1, N_PAIRS, tm, LANES), lambda b, i: (b, 0, i, 0))
    return pl.pallas_call(
        _rwkv_out_kernel,
        grid=(bx, tx // tm),
        in_specs=[pair, pair, tok, tok, tok, _mod_spec(row0, rowstride),
                  _full(gains.shape), _full(lnx.shape), _full(w_o.shape), _full(ones_bd.shape)],
        out_specs=tok,
        out_shape=jax.ShapeDtypeStruct(x.shape, F32),
        compiler_params=_params("parallel", "parallel"),
        name="rwkv_out",
    )(yf, yb, bonus, gate, x, mod3, gains, lnx, w_o, ones_bd)


def _rope_tables(t_len):
    rows = t_len // GRID_W
    row = jnp.repeat(jnp.arange(rows), GRID_W).astype(F32)
    col = jnp.tile(jnp.arange(GRID_W), rows).astype(F32)
    inv = 1.0 / (ROPE_THETA ** (jnp.arange(0, AXIS_DIM, 2, dtype=F32) / AXIS_DIM))
    ar = row[:, None] * inv[None, :]
    ac = col[:, None] * inv[None, :]
    ang = jnp.concatenate([ar, ar, ac, ac], axis=-1)
    cos, sin = jnp.cos(ang), jnp.sin(ang)
    first = (jnp.arange(HEAD_DIM) % AXIS_DIM) < AXIS_DIM // 2
    sin_a = jnp.where(first, -sin, 0.0)
    sin_b = jnp.where(first, 0.0, sin)
    rep = LANES // HEAD_DIM
    return tuple(jnp.tile(t, (1, rep)) for t in (cos, sin_a, sin_b))


def _dup_heads(t):
    lead = t.shape[:-1]
    t = t.reshape(lead + (GQA_KV_HEADS, 1, HEAD_DIM))
    return jnp.broadcast_to(t, lead + (GQA_KV_HEADS, LANES // HEAD_DIM, HEAD_DIM)).reshape(lead + (GQA_KV_HEADS * LANES,))


def _block_diag_pairs(s):
    b = s.shape[0]
    s = s.reshape(b, 2, N_PAIRS, 2, HEAD_DIM, HEAD_DIM)
    z = jnp.zeros_like(s[:, :, :, 0])
    top = jnp.concatenate([s[:, :, :, 0], z], axis=-1)
    bot = jnp.concatenate([z, s[:, :, :, 1]], axis=-1)
    return jnp.concatenate([top, bot], axis=-2)


def kernel(x_prompt, x_sample, c, cache_k_gqa, cache_v_gqa, cache_k_diff, cache_v_diff, state_rwkv, c_ctx, w_ada, b_ada, norm_gains, attn_w_in, attn_w_out, attn_qk_gain, diff_lambda, diff_subln, rwkv_mu, rwkv_w_rkv, rwkv_w_o, rwkv_w0, rwkv_w1, rwkv_w2, rwkv_a0, rwkv_a1, rwkv_a2, rwkv_g1, rwkv_g2, rwkv_kvec, rwkv_lnx, mlp_w1, mlp_w2):
    d = D_MODEL
    nb, seq, _ = x_prompt.shape
    nd, dseq, _ = x_sample.shape
    cond = jnp.concatenate([c_ctx[None, :], c, jnp.zeros((COND_ROWS - 1 - nd, d), F32)], axis=0)
    mod_all = _adaln(cond, w_ada, b_ada)
    hd_ones = np.kron(np.eye(GQA_Q // HEAD_DIM, dtype=np.float32), np.ones((HEAD_DIM, HEAD_DIM), np.float32))
    ones_bd = jnp.asarray(hd_ones, BF16)
    rope_tabs = _rope_tables(dseq)
    xp, xs = x_prompt, x_sample
    streams = lambda: ((xp, 0, 0), (xs, 1, 1))
    kg, vg, kdf, vdf, st = [], [], [], [], []
    for i in range(DEPTH):
        j = i // 2
        mod3 = mod_all[i].reshape(COND_ROWS, 1, 6 * d)
        gains = norm_gains[i]
        if i % 2 == 0:
            lam_init = 0.8 - 0.6 * math.exp(-0.3 * i)
            w_in = attn_w_in[j].astype(BF16)
            w_out = attn_w_out[j].astype(BF16)
            gq = jnp.tile(attn_qk_gain[j, 0], GQA_HEADS)[None, :]
            gk = jnp.tile(attn_qk_gain[j, 1], GQA_KV_HEADS)[None, :]
            subln = diff_subln[j][None, :]
            qa, ka, va, qb, kb, vb = _attn_in(xp, mod3, 0, 0, gains, w_in, gq, gk, ones_bd, None)
            kg.append(ka.reshape(nb, seq, GQA_KV_HEADS, HEAD_DIM))
            vg.append(va.reshape(nb, seq, GQA_KV_HEADS, HEAD_DIM))
            kdf.append(kb.reshape(nb, seq, DIFF_HEADS, 2, HEAD_DIM))
            vdf.append(vb.reshape(nb, seq, DIFF_HEADS, 2 * HEAD_DIM))
            mix = _attention(qa, qb, _dup_heads(ka).astype(BF16), _dup_heads(va).astype(BF16),
                             kb.astype(BF16), vb.astype(BF16), diff_lambda[j], subln, lam_init)
            xp = _out_proj(mix, xp, mod3, 0, 0, gains, w_out)
            qa, ka, va, qb, kb, vb = _attn_in(xs, mod3, 1, 1, gains, w_in, gq, gk, ones_bd, rope_tabs)
            past = cache_k_gqa.shape[2]
            cat = lambda cached, new: jnp.concatenate([cached.astype(BF16), new.astype(BF16)], axis=1)
            ka2 = cat(_dup_heads(cache_k_gqa[:, j].reshape(nd, past, GQA_KV)), _dup_heads(ka))
            va2 = cat(_dup_heads(cache_v_gqa[:, j].reshape(nd, past, GQA_KV)), _dup_heads(va))
            kb2 = cat(cache_k_diff[:, j].reshape(nd, past, DIFF_QK), kb)
            vb2 = cat(cache_v_diff[:, j].reshape(nd, past, DIFF_V), vb)
            mix = _attention(qa, qb, ka2, va2, kb2, vb2, diff_lambda[j], subln, lam_init)
            xs = _out_proj(mix, xs, mod3, 1, 1, gains, w_out)
        else:
            wrkv = rwkv_w_rkv[j].astype(BF16)
            wlo = jnp.stack([jnp.concatenate([rwkv_w1[j, 0], rwkv_w1[j, 1]], axis=-1),
                             jnp.concatenate([rwkv_a1[j, 0], rwkv_a1[j, 1]], axis=-1),
                             rwkv_g1[j]]).astype(BF16)
            zlo = jnp.zeros_like(rwkv_w2[j, 0])
            w2bd = jnp.concatenate([jnp.concatenate([rwkv_w2[j, 0], zlo], axis=-1),
                                    jnp.concatenate([zlo, rwkv_w2[j, 1]], axis=-1)], axis=0).astype(BF16)
            a2bd = jnp.concatenate([jnp.concatenate([rwkv_a2[j, 0], zlo], axis=-1),
                                    jnp.concatenate([zlo, rwkv_a2[j, 1]], axis=-1)], axis=0).astype(BF16)
            g2 = rwkv_g2[j].astype(BF16)
            w_o = rwkv_w_o[j].astype(BF16)
            new = []
            for x, row0, rowstride in streams():
                sample = row0 == 1
                r, v, kk, gate, bonus, lw, kd, ka = _rwkv_in(
                    x, mod3, row0, rowstride, gains, rwkv_mu[j], wrkv, wlo, w2bd, a2bd, g2,
                    rwkv_w0[j], rwkv_a0[j], rwkv_kvec[j], ones_bd, x.shape[1])
                s0 = _block_diag_pairs(state_rwkv[:, j]) if sample else None
                res = _wkv(r, v, kk, lw, kd, ka, s0, want_final=not sample)
                if not sample:
                    st.append(res[2])
                new.append(_rwkv_out(res[0], res[1], bonus, gate, x, mod3, row0, rowstride, gains,
                                     rwkv_lnx[j], w_o, ones_bd))
            xp, xs = new
        w1 = mlp_w1[i].astype(BF16)
        w2 = mlp_w2[i].astype(BF16)
        xp = _mlp(xp.reshape(nb * seq // 512, 512, d), mod3, 0, 0, gains, w1, w2).reshape(nb, seq, d)
        xs = _mlp(xs, mod3, 1, 1, gains, w1, w2)
    return (xp, xs, jnp.stack(kg, axis=1), jnp.stack(vg, axis=1), jnp.stack(kdf, axis=1), jnp.stack(vdf, axis=1),
            jnp.stack(st, axis=1))
```

```python
import functools
import math

import numpy as np
import jax
import jax.numpy as jnp
from jax import lax
from jax.experimental import pallas as pl
from jax.experimental.pallas import tpu as pltpu

F32 = jnp.float32
BF16 = jnp.bfloat16

D_MODEL = 1024
DEPTH = 4
GRID_W = 64
HEAD_DIM = 64
AXIS_DIM = HEAD_DIM // 2
ROPE_THETA = 10000.0
GQA_HEADS = 8
GQA_KV_HEADS = 2
DIFF_HEADS = 4
RWKV_HEADS = D_MODEL // HEAD_DIM
D_FF = 4 * D_MODEL
GQA_Q = GQA_HEADS * HEAD_DIM
GQA_KV = GQA_KV_HEADS * HEAD_DIM
DIFF_QK = DIFF_HEADS * 2 * HEAD_DIM
DIFF_V = DIFF_HEADS * 2 * HEAD_DIM
ATTN_IN = GQA_Q + 2 * GQA_KV + 2 * DIFF_QK + DIFF_V
NORM_EPS = 1e-6
LNX_EPS = 64e-5

LANES = 128
SUBLANES = 8
N_PAIRS = D_MODEL // LANES
CHUNK = 64
COND_ROWS = 8
VMEM_LIMIT = 56 * 1024 * 1024


def _dot(a, b):
    return jnp.dot(a.astype(BF16), b.astype(BF16), preferred_element_type=F32)


def _dot_nt(a, b):
    return lax.dot_general(a.astype(BF16), b.astype(BF16), (((1,), (1,)), ((), ())), preferred_element_type=F32)


def _dot_tn(a, b):
    return lax.dot_general(a.astype(BF16), b.astype(BF16), (((0,), (0,)), ((), ())), preferred_element_type=F32)


def _bdot(a, b):
    return lax.dot_general(a.astype(BF16), b.astype(BF16), (((2,), (1,)), ((0,), (0,))), preferred_element_type=F32)


def _bdot_nt(a, b):
    return lax.dot_general(a.astype(BF16), b.astype(BF16), (((2,), (2,)), ((0,), (0,))), preferred_element_type=F32)


def _bdot_tn(a, b):
    return jnp.stack([_dot_tn(a[g], b[g]) for g in range(a.shape[0])])


def _rms(x):
    return x * lax.rsqrt(jnp.mean(x * x, axis=-1, keepdims=True) + NORM_EPS)


def _sigmoid(x):
    return 1.0 / (1.0 + jnp.exp(-x))


def _params(*sem):
    return pltpu.CompilerParams(dimension_semantics=sem, vmem_limit_bytes=VMEM_LIMIT)


def _full(arr):
    nd = arr.ndim
    return pl.BlockSpec(arr.shape, lambda *_: (0,) * nd)


def _layer(arr, j):
    nd = arr.ndim
    return pl.BlockSpec((1,) + arr.shape[1:], lambda *_: (j,) + (0,) * (nd - 1))


def _mod_spec(layer, row0, rowstride):
    return pl.BlockSpec((1, 1, 6 * D_MODEL), lambda b, i: (layer * COND_ROWS + row0 + b * rowstride, 0, 0))


def _tok(tm, w):
    return pl.BlockSpec((1, tm, w), lambda b, i: (b, i, 0))


def _adaln_kernel(c_ref, w_ref, b_ref, o_ref):
    c = c_ref[...]
    o_ref[0] = _dot(c * _sigmoid(c), w_ref[0]) + b_ref[0]


def _adaln(cond, w_ada, b_ada):
    tn = 1536
    return pl.pallas_call(
        _adaln_kernel,
        grid=(DEPTH, 6 * D_MODEL // tn),
        in_specs=[_full(cond),
                  pl.BlockSpec((1, D_MODEL, tn), lambda l, j: (l, 0, j)),
                  pl.BlockSpec((1, 1, tn), lambda l, j: (l, 0, j))],
        out_specs=pl.BlockSpec((1, COND_ROWS, tn), lambda l, j: (l, 0, j)),
        out_shape=jax.ShapeDtypeStruct((DEPTH, COND_ROWS, 6 * D_MODEL), F32),
        compiler_params=_params("parallel", "parallel"),
        name="adaln",
    )(cond, w_ada, b_ada.reshape(DEPTH, 1, 6 * D_MODEL))


def _head_ms(x, ones_bd):
    w = x.shape[-1]
    return _dot(x * x, ones_bd[0:w, 0:w]) * (1.0 / HEAD_DIM)


def _rope(x, cos, sin_a, sin_b):
    cols = []
    for j in range(x.shape[-1] // LANES):
        xj = x[:, j * LANES:(j + 1) * LANES]
        cols.append(xj * cos + pltpu.roll(xj, LANES - AXIS_DIM // 2, 1) * sin_a + pltpu.roll(xj, AXIS_DIM // 2, 1) * sin_b)
    return cols[0] if len(cols) == 1 else jnp.concatenate(cols, axis=-1)


def _dup_kv_heads(t, lo):
    swapped = pltpu.roll(t, HEAD_DIM, 1)
    return jnp.concatenate([jnp.where(lo, t, swapped), jnp.where(lo, swapped, t)], axis=-1)


def _attn_in_kernel(*refs, rope, side):
    x_ref, mod_ref, g_ref, w_ref, gq_ref, gk_ref, ones_ref = refs[:7]
    rest = list(refs[7:])
    tabs = [rest.pop(0) for _ in range(3)] if rope else None
    qa_ref, qb_ref, kdup_ref, vdup_ref, kb_ref, vb_ref = rest[:6]
    d = D_MODEL
    mod = mod_ref[0]
    h = _rms(x_ref[0]) * g_ref[0, 0:1, :] * (1.0 + mod[:, d:2 * d]) + mod[:, 0:d]
    z = _dot(h, w_ref[0])
    o = 0
    qa = z[:, o:o + GQA_Q]; o += GQA_Q
    ka = z[:, o:o + GQA_KV]; o += GQA_KV
    va = z[:, o:o + GQA_KV]; o += GQA_KV
    qb = z[:, o:o + DIFF_QK]; o += DIFF_QK
    kb = z[:, o:o + DIFF_QK]; o += DIFF_QK
    vb = z[:, o:o + DIFF_V]
    ones_bd = ones_ref[...]
    qa = qa * lax.rsqrt(_head_ms(qa, ones_bd) + NORM_EPS) * gq_ref[0]
    ka = ka * lax.rsqrt(_head_ms(ka, ones_bd) + NORM_EPS) * gk_ref[0]
    if side:
        ka_ref, va_ref, kbf_ref, vbf_ref = rest[6:]
        ka_ref[0] = ka
        va_ref[0] = va
        kbf_ref[0] = kb
        vbf_ref[0] = vb
    if rope:
        cos, sa, sb = (t[...] for t in tabs)
        qa, ka, qb, kb = (_rope(t, cos, sa, sb) for t in (qa, ka, qb, kb))
    scale = HEAD_DIM ** -0.5
    lo = lax.broadcasted_iota(jnp.int32, (1, LANES), 1) < HEAD_DIM
    qa_ref[0] = (qa * scale).astype(BF16)
    qb_ref[0] = (qb * scale).astype(BF16)
    kdup_ref[0] = _dup_kv_heads(ka, lo).astype(BF16)
    vdup_ref[0] = _dup_kv_heads(va, lo).astype(BF16)
    kb_ref[0] = kb.astype(BF16)
    vb_ref[0] = vb.astype(BF16)


def _attn_in(x, mod, layer, row0, rowstride, gains, j, w_in, gq, gk, ones_bd, rope_tabs, side, tm=256):
    bx, tx, d = x.shape
    widths = (GQA_Q, DIFF_QK, 2 * LANES, 2 * LANES, DIFF_QK, DIFF_V)
    in_specs = [_tok(tm, d), _mod_spec(layer, row0, rowstride), _layer(gains, layer), _layer(w_in, j), _layer(gq, j),
                _layer(gk, j), _full(ones_bd)]
    args = [x, mod, gains, w_in, gq, gk, ones_bd]
    if rope_tabs is not None:
        in_specs += [pl.BlockSpec((tm, LANES), lambda b, i: (i, 0))] * 3
        args += list(rope_tabs)
    out_specs = [_tok(tm, w) for w in widths]
    out_shape = [jax.ShapeDtypeStruct((bx, tx, w), BF16) for w in widths]
    if side:
        side_widths = (GQA_KV, GQA_KV, DIFF_QK, DIFF_V)
        out_specs += [_tok(tm, w) for w in side_widths]
        out_shape += [jax.ShapeDtypeStruct((bx, tx, w), F32) for w in side_widths]
    return pl.pallas_call(
        functools.partial(_attn_in_kernel, rope=rope_tabs is not None, side=side),
        grid=(bx, tx // tm),
        in_specs=in_specs,
        out_specs=out_specs,
        out_shape=out_shape,
        compiler_params=_params("parallel", "parallel"),
        name="attn_in",
    )(*args)


def _softmax_pv(q, ks, vs):
    ss = [_dot_nt(q, k) for k in ks]
    m = functools.reduce(jnp.maximum, [jnp.max(s, axis=-1, keepdims=True) for s in ss])
    ps = [jnp.exp(s - m) for s in ss]
    inv = 1.0 / sum(jnp.sum(p, axis=-1, keepdims=True) for p in ps)
    return ps, inv


def _attn_kernel(*refs, lam_init, cached):
    qa_ref, qb_ref = refs[:2]
    n_seg = 2 if cached else 1
    segs = refs[2:2 + 4 * n_seg]
    ka_refs, va_refs, kb_refs, vb_refs = (segs[i::4] for i in range(4))
    lam_ref, sub_ref, o_ref = refs[2 + 4 * n_seg:]
    lo = lax.broadcasted_iota(jnp.int32, (1, LANES), 1) < HEAD_DIM
    hi = jnp.logical_not(lo)
    for g in range(GQA_KV_HEADS):
        ks = [r[0, :, g * LANES:(g + 1) * LANES] for r in ka_refs]
        vs = [r[0, :, g * LANES:(g + 1) * LANES] for r in va_refs]
        for j in range(GQA_HEADS // GQA_KV_HEADS // 2):
            c0 = (g * (GQA_HEADS // GQA_KV_HEADS // 2) + j) * LANES
            q = qa_ref[0, :, c0:c0 + LANES]
            halves = []
            for keep in (lo, hi):
                ps, inv = _softmax_pv(jnp.where(keep, q, jnp.zeros_like(q)), ks, vs)
                halves.append(sum(_dot(p, v) for p, v in zip(ps, vs)) * inv)
            o_ref[0, :, c0:c0 + LANES] = jnp.where(lo, halves[0], halves[1])
    lam4 = lam_ref[0]
    lam = (jnp.exp(jnp.sum(lam4[0:1] * lam4[1:2], axis=-1, keepdims=True))
           - jnp.exp(jnp.sum(lam4[2:3] * lam4[3:4], axis=-1, keepdims=True)) + lam_init)
    for hd in range(DIFF_HEADS):
        c0 = hd * LANES
        q = qb_ref[0, :, c0:c0 + LANES]
        ks = [r[0, :, c0:c0 + LANES] for r in kb_refs]
        vs = [r[0, :, c0:c0 + LANES] for r in vb_refs]
        p0, inv0 = _softmax_pv(jnp.where(lo, q, jnp.zeros_like(q)), ks, vs)
        p1, inv1 = _softmax_pv(jnp.where(hi, q, jnp.zeros_like(q)), ks, vs)
        o = sum(_dot(a * inv0 - lam * (b * inv1), v) for a, b, v in zip(p0, p1, vs))
        o_ref[0, :, GQA_Q + c0:GQA_Q + c0 + LANES] = _rms(o) * sub_ref[0] * (1.0 - lam_init)


def _attention(qa, qb, new_kv, cache_kv, j, lam, subln, lam_init, tq=256):
    bx, tx, _ = qa.shape
    seg_specs, seg_args = [], []
    for t in new_kv:
        seg_specs.append(pl.BlockSpec((1, tx, t.shape[-1]), lambda b, i: (b, 0, 0)))
        seg_args.append(t)
    if cache_kv is not None:
        for t in cache_kv:
            seg_specs.append(pl.BlockSpec((1, None) + t.shape[2:], lambda b, i: (b, j, 0, 0)))
            seg_args.append(t)
    return pl.pallas_call(
        functools.partial(_attn_kernel, lam_init=lam_init, cached=cache_kv is not None),
        grid=(bx, tx // tq),
        in_specs=[_tok(tq, GQA_Q), _tok(tq, DIFF_QK)] + seg_specs + [_layer(lam, j), _layer(subln, j)],
        out_specs=_tok(tq, D_MODEL),
        out_shape=jax.ShapeDtypeStruct((bx, tx, D_MODEL), F32),
        compiler_params=_params("parallel", "parallel"),
        name="attention",
    )(qa, qb, *seg_args, lam, subln)


def _out_proj_kernel(m_ref, x_ref, mod_ref, g_ref, w_ref, o_ref):
    d = D_MODEL
    gate = mod_ref[0][:, 2 * d:3 * d]
    o_ref[0] = x_ref[0] + gate * (_rms(_dot(m_ref[0], w_ref[0])) * g_ref[0, 1:2, :])


def _out_proj(mix, x, mod, layer, row0, rowstride, gains, j, w, tm=256):
    bx, tx, d = x.shape
    return pl.pallas_call(
        _out_proj_kernel,
        grid=(bx, tx // tm),
        in_specs=[_tok(tm, d), _tok(tm, d), _mod_spec(layer, row0, rowstride), _layer(gains, layer), _layer(w, j)],
        out_specs=_tok(tm, d),
        out_shape=jax.ShapeDtypeStruct(x.shape, F32),
        compiler_params=_params("parallel", "parallel"),
        name="out_proj",
    )(mix, x, mod, gains, w)


def _mlp_kernel(x_ref, mod_ref, g_ref, w1_ref, w2_ref, o_ref, *, fc):
    d = D_MODEL
    x = x_ref[0]
    mod = mod_ref[0]
    h = (_rms(x) * g_ref[0, 2:3, :] * (1.0 + mod[:, 4 * d:5 * d]) + mod[:, 3 * d:4 * d]).astype(BF16)
    acc = jnp.zeros(x.shape, F32)
    for c in range(D_FF // fc):
        u = jnp.maximum(jnp.dot(h, w1_ref[0, :, c * fc:(c + 1) * fc], preferred_element_type=F32), 0.0)
        acc = acc + _dot(u * u, w2_ref[0, c * fc:(c + 1) * fc, :])
    o_ref[0] = x + mod[:, 5 * d:6 * d] * (_rms(acc) * g_ref[0, 3:4, :])


def _mlp(x, mod, layer, row0, rowstride, gains, w1, w2, tm=512, fc=512):
    bx, tx, d = x.shape
    return pl.pallas_call(
        functools.partial(_mlp_kernel, fc=fc),
        grid=(bx, tx // tm),
        in_specs=[_tok(tm, d), _mod_spec(layer, row0, rowstride), _layer(gains, layer), _layer(w1, layer),
                  _layer(w2, layer)],
        out_specs=_tok(tm, d),
        out_shape=jax.ShapeDtypeStruct(x.shape, F32),
        compiler_params=_params("parallel", "parallel"),
        name="mlp",
    )(x, mod, gains, w1, w2)


def _head_sum_signed(t, ones_bd):
    w = 2 * LANES
    hi = t.astype(BF16)
    return jnp.concatenate(
        [jnp.dot(hi[:, j:j + w], ones_bd[0:w, 0:w], preferred_element_type=F32)
         + _dot(t[:, j:j + w] - hi[:, j:j + w].astype(F32), ones_bd[0:w, 0:w]) for j in range(0, t.shape[-1], w)],
        axis=-1)


def _head_ms_wide(x, ones_bd):
    w = 2 * LANES
    return jnp.concatenate([_dot(x[:, j:j + w] * x[:, j:j + w], ones_bd[0:w, 0:w]) for j in range(0, x.shape[-1], w)],
                           axis=-1) * (1.0 / HEAD_DIM)


def _rwkv_in_kernel(x_ref, xp_ref, xn_ref, mod_ref, g_ref, mu_ref, wrkv_ref, wlo_ref, w2_ref, a2_ref, g2_ref,
                    w0_ref, a0_ref, kv_ref, ones_ref,
                    r_ref, v_ref, kk_ref, gate_ref, bonus_ref, lw_ref, kd_ref, ka_ref, hs_ref, *, seq_len):
    d = D_MODEL
    tm = x_ref.shape[1]
    mod = mod_ref[0]
    g0 = g_ref[0, 0:1, :]
    sc = 1.0 + mod[:, d:2 * d]
    sh = mod[:, 0:d]
    norm = lambda t: _rms(t) * g0 * sc + sh
    h = norm(x_ref[0])
    hs_ref[SUBLANES:SUBLANES + tm, :] = h
    hs_ref[0:SUBLANES, :] = norm(xp_ref[0])
    hs_ref[SUBLANES + tm:2 * SUBLANES + tm, :] = norm(xn_ref[0])
    row = lax.broadcasted_iota(jnp.int32, (tm, 1), 0)
    t_seq = (pl.program_id(1) * tm + row) % seq_len
    prev = jnp.where(t_seq == 0, 0.0, hs_ref[SUBLANES - 1:SUBLANES - 1 + tm, :])
    nxt = jnp.where(t_seq == seq_len - 1, 0.0, hs_ref[SUBLANES + 1:SUBLANES + 1 + tm, :])
    xx = 0.5 * (prev + nxt) - h
    xr, xw, xk, xv, xa, xg = (h + xx * mu_ref[0, n:n + 1, :] for n in range(6))
    r = _dot(xr, wrkv_ref[0, 0])
    k = _dot(xk, wrkv_ref[0, 1])
    v = _dot(xv, wrkv_ref[0, 2])
    gate = _dot(_sigmoid(_dot(xg, wlo_ref[0, 2])), g2_ref[0])
    wl = _dot(jnp.tanh(_dot(xw, wlo_ref[0, 0])), w2_ref[0])
    al = _dot(_dot(xa, wlo_ref[0, 1]), a2_ref[0])
    ones_bd = ones_ref[...]
    k_k, k_a, r_k = kv_ref[0, 0:1, :], kv_ref[0, 1:2, :], kv_ref[0, 2:3, :]
    kk = k * k_k
    kk = kk * lax.rsqrt(_head_ms_wide(kk, ones_bd) * HEAD_DIM + 1e-12)
    bonus = jnp.zeros_like(v)
    for dr in range(2):
        lw = (-math.exp(-0.5)) * _sigmoid(w0_ref[0, dr:dr + 1, :] + wl[:, dr * d:(dr + 1) * d])
        a = _sigmoid(a0_ref[0, dr:dr + 1, :] + al[:, dr * d:(dr + 1) * d])
        kd = k * (1.0 + (a - 1.0) * k_a)
        ka = kk * a
        bonus = bonus + _head_sum_signed(r * kd * r_k, ones_bd) * v
        for p in range(N_PAIRS):
            sl = slice(p * LANES, (p + 1) * LANES)
            lw_ref[dr, 0, p] = lw[:, sl]
            kd_ref[dr, 0, p] = kd[:, sl]
            ka_ref[dr, 0, p] = ka[:, sl]
    for p in range(N_PAIRS):
        sl = slice(p * LANES, (p + 1) * LANES)
        r_ref[0, p] = r[:, sl]
        v_ref[0, p] = v[:, sl]
        kk_ref[0, p] = kk[:, sl]
    gate_ref[0] = gate
    bonus_ref[0] = bonus


def _rwkv_in(x, mod, layer, row0, rowstride, gains, j, consts, ones_bd, seq_len, tm=256):
    bx, tx, d = x.shape
    r8 = tm // SUBLANES
    pair = jax.ShapeDtypeStruct((bx, N_PAIRS, tx, LANES), F32)
    pair2 = jax.ShapeDtypeStruct((2, bx, N_PAIRS, tx, LANES), F32)
    tok = jax.ShapeDtypeStruct((bx, tx, d), F32)
    pair_spec = pl.BlockSpec((1, N_PAIRS, tm, LANES), lambda b, i: (b, 0, i, 0))
    pair2_spec = pl.BlockSpec((2, 1, N_PAIRS, tm, LANES), lambda b, i: (0, b, 0, i, 0))
    return pl.pallas_call(
        functools.partial(_rwkv_in_kernel, seq_len=seq_len),
        grid=(bx, tx // tm),
        in_specs=[_tok(tm, d),
                  pl.BlockSpec((1, SUBLANES, d), lambda b, i: (b, jnp.maximum(i * r8 - 1, 0), 0)),
                  pl.BlockSpec((1, SUBLANES, d), lambda b, i: (b, jnp.minimum((i + 1) * r8, tx // SUBLANES - 1), 0)),
                  _mod_spec(layer, row0, rowstride), _layer(gains, layer)]
                 + [_layer(c, j) for c in consts] + [_full(ones_bd)],
        out_specs=[pair_spec, pair_spec, pair_spec, _tok(tm, d), _tok(tm, d), pair2_spec, pair2_spec, pair2_spec],
        out_shape=[pair, pair, pair, tok, tok, pair2, pair2, pair2],
        scratch_shapes=[pltpu.VMEM((tm + 2 * SUBLANES, d), F32)],
        compiler_params=_params("parallel", "parallel"),
        name="rwkv_in",
    )(x, x, x, mod, gains, *consts, ones_bd)


def _wkv_masks():
    n2 = 2 * CHUNK
    idx = np.arange(n2)
    order64, half64, tri, lvl = [], [], [], []
    for rev in (False, True):
        pos = (n2 - 1 - idx) if rev else idx
        pt, ps = pos[:, None], pos[None, :]
        same = (pt // CHUNK) == (ps // CHUNK)
        tri.append(np.concatenate([same & (ps < pt), same & (ps <= pt)], axis=0))
        mats = []
        n = 1
        while n < CHUNK:
            mats.append((pt // (2 * n) == ps // (2 * n)) & ((pt // n) % 2 == 1) & ((ps // n) % 2 == 0))
            n *= 2
        lvl.append(np.stack(mats))
        p64 = (CHUNK - 1 - np.arange(CHUNK)) if rev else np.arange(CHUNK)
        order64.append(p64[None, :] <= p64[:, None])
        half64.append((p64 < CHUNK // 2)[:, None])
    as32 = lambda m: np.stack(m).astype(np.float32)
    return as32(order64), as32(half64), as32(tri), as32(lvl)


def _wkv_chunk(r, v, kk, lw, kd, ka, s_bd, order, half, tri, lvl):
    g = 2 * N_PAIRS
    n2 = 2 * CHUNK

    def per_dir(x, m):
        return (x.reshape((2, N_PAIRS) + x.shape[1:]) * m[:, None]).reshape(x.shape)

    lo = lax.broadcasted_iota(jnp.int32, (1, 1, LANES), 2) < HEAD_DIM
    stack = lambda t: jnp.concatenate([jnp.where(lo, t, 0.0), jnp.where(lo, 0.0, t)], axis=1)
    order_g = jnp.broadcast_to(order[:, None], (2, N_PAIRS, CHUNK, CHUNK)).reshape(g, CHUNK, CHUNK).astype(BF16)
    lw_hi = lw.astype(BF16)
    cum = _bdot(order_g, lw_hi) + _bdot(order_g, lw - lw_hi.astype(F32))
    tot = jnp.sum(lw, axis=1, keepdims=True)
    mid = jnp.sum(per_dir(lw, half), axis=1, keepdims=True)
    g_inv = jnp.exp(mid - cum)
    g_tail = jnp.exp(tot - cum)
    reads = jnp.concatenate([stack(kk * jnp.exp(cum - lw - mid)), stack(r * jnp.exp(cum - mid))], axis=1)
    ai = ka * g_inv
    ki = kd * g_inv
    gram = _bdot_nt(reads, jnp.concatenate([ai, ai, ki, ki], axis=1))
    ac = per_dir(gram[:, :, 0:n2], tri)
    bd = per_dir(gram[:, :, n2:2 * n2], tri)
    a = ac[:, 0:n2]
    eye = (lax.broadcasted_iota(jnp.int32, (1, n2, n2), 1) == lax.broadcasted_iota(jnp.int32, (1, n2, n2), 2)).astype(F32)
    tinv = eye - per_dir(a, lvl[:, 0])
    for level in range(1, lvl.shape[1]):
        tinv = tinv - _bdot(tinv, _bdot(per_dir(a, lvl[:, level]), tinv))
    v2 = stack(v)
    from_state = _bdot_nt(reads, s_bd * jnp.exp(mid))
    from_v = _bdot(bd, v2)
    u2 = _bdot(tinv, from_state[:, 0:n2] + from_v[:, 0:n2])
    y2 = from_state[:, n2:] + from_v[:, n2:] - _bdot(ac[:, n2:], u2)
    s_new = s_bd * jnp.exp(tot) + _bdot_tn(jnp.concatenate([v2, u2], axis=1),
                                           jnp.concatenate([stack(kd * g_tail), stack(-(ka * g_tail))], axis=1))
    return y2[:, 0:CHUNK] + y2[:, CHUNK:n2], s_new


def _wkv_kernel(*refs, has_init, want_final):
    (rf_ref, vf_ref, kkf_ref, rb_ref, vb_ref, kkb_ref, lwf_ref, kdf_ref, kaf_ref, lwb_ref, kdb_ref, kab_ref,
     order_ref, half_ref, tri_ref, lvl_ref) = refs[:16]
    rest = list(refs[16:])
    s0_ref = rest.pop(0) if has_init else None
    yf_ref = rest.pop(0)
    yb_ref = rest.pop(0)
    sf_ref = rest.pop(0) if want_final else None
    s_scr = rest.pop(0)
    c = pl.program_id(1)
    nc = pl.num_programs(1)

    @pl.when(c == 0)
    def _():
        if has_init:
            s_scr[...] = s0_ref[0, 0]
        else:
            s_scr[...] = jnp.zeros(s_scr.shape, F32)

    both = lambda f_ref, b_ref: jnp.concatenate([f_ref[...].reshape(N_PAIRS, CHUNK, LANES),
                                                 b_ref[...].reshape(N_PAIRS, CHUNK, LANES)], axis=0)
    y, s_new = _wkv_chunk(both(rf_ref, rb_ref), both(vf_ref, vb_ref), both(kkf_ref, kkb_ref), both(lwf_ref, lwb_ref),
                          both(kdf_ref, kdb_ref), both(kaf_ref, kab_ref), s_scr[...],
                          order_ref[...], half_ref[...], tri_ref[...], lvl_ref[...])
    yf_ref[0] = y[0:N_PAIRS]
    yb_ref[0] = y[N_PAIRS:]
    s_scr[...] = s_new

    if want_final:
        @pl.when(c == nc - 1)
        def _():
            for dr in range(2):
                for p in range(N_PAIRS):
                    s = s_scr[dr * N_PAIRS + p]
                    sf_ref[0, dr, 2 * p] = s[0:HEAD_DIM, 0:HEAD_DIM]
                    sf_ref[0, dr, 2 * p + 1] = s[HEAD_DIM:, HEAD_DIM:]


def _wkv(r, v, kk, lw, kd, ka, s0_bd, j, want_final):
    bx, _, tx, _ = r.shape
    nc = tx // CHUNK
    masks = tuple(jnp.asarray(m) for m in _wkv_masks())
    fwd = pl.BlockSpec((1, N_PAIRS, CHUNK, LANES), lambda b, c: (b, 0, c, 0))
    bwd = pl.BlockSpec((1, N_PAIRS, CHUNK, LANES), lambda b, c: (b, 0, nc - 1 - c, 0))
    fwd2 = pl.BlockSpec((1, 1, N_PAIRS, CHUNK, LANES), lambda b, c: (0, b, 0, c, 0))
    bwd2 = pl.BlockSpec((1, 1, N_PAIRS, CHUNK, LANES), lambda b, c: (1, b, 0, nc - 1 - c, 0))
    in_specs = [fwd, fwd, fwd, bwd, bwd, bwd, fwd2, fwd2, fwd2, bwd2, bwd2, bwd2] + [_full(m) for m in masks]
    args = [r, v, kk, r, v, kk, lw, kd, ka, lw, kd, ka, *masks]
    if s0_bd is not None:
        in_specs.append(pl.BlockSpec((1, 1, 2 * N_PAIRS, LANES, LANES), lambda b, c: (b, j, 0, 0, 0)))
        args.append(s0_bd)
    y_shape = jax.ShapeDtypeStruct(r.shape, F32)
    out_specs = [fwd, bwd]
    out_shape = [y_shape, y_shape]
    if want_final:
        out_specs.append(pl.BlockSpec((1, 2, RWKV_HEADS, HEAD_DIM, HEAD_DIM), lambda b, c: (b, 0, 0, 0, 0)))
        out_shape.append(jax.ShapeDtypeStruct((bx, 2, RWKV_HEADS, HEAD_DIM, HEAD_DIM), F32))
    return pl.pallas_call(
        functools.partial(_wkv_kernel, has_init=s0_bd is not None, want_final=want_final),
        grid=(bx, nc),
        in_specs=in_specs,
        out_specs=out_specs,
        out_shape=out_shape,
        scratch_shapes=[pltpu.VMEM((2 * N_PAIRS, LANES, LANES), F32)],
        compiler_params=_params("parallel", "arbitrary"),
        name="wkv",
    )(*args)


def _rwkv_out_kernel(yf_ref, yb_ref, bonus_ref, gate_ref, x_ref, mod_ref, g_ref, lnx_ref, w_ref, ones_ref, o_ref):
    d = D_MODEL
    ones_bd = ones_ref[...]
    y = jnp.concatenate([yf_ref[0, p] + yb_ref[0, p] for p in range(N_PAIRS)], axis=-1)
    yc = y - _head_sum_signed(y, ones_bd) * (1.0 / HEAD_DIM)
    yn = yc * lax.rsqrt(_head_ms_wide(yc, ones_bd) + LNX_EPS)
    z = (yn * lnx_ref[0, 0:1, :] + lnx_ref[0, 1:2, :] + bonus_ref[0]) * gate_ref[0]
    gate1 = mod_ref[0][:, 2 * d:3 * d]
    o_ref[0] = x_ref[0] + gate1 * (_rms(_dot(z, w_ref[0])) * g_ref[0, 1:2, :])


def _rwkv_out(yf, yb, bonus, gate, x, mod, layer, row0, rowstride, gains, j, lnx, w_o, ones_bd, tm=256):
    bx, tx, d = x.shape
    pair = pl.BlockSpec((1, N_PAIRS, tm, LANES), lambda b, i: (b, 0, i, 0))
    return pl.pallas_call(
        _rwkv_out_kernel,
        grid=(bx, tx // tm),
        in_specs=[pair, pair, _tok(tm, d), _tok(tm, d), _tok(tm, d), _mod_spec(layer, row0, rowstride),
                  _layer(gains, layer), _layer(lnx, j), _layer(w_o, j), _full(ones_bd)],
        out_specs=_tok(tm, d),
        out_shape=jax.ShapeDtypeStruct(x.shape, F32),
        compiler_params=_params("parallel", "parallel"),
        name="rwkv_out",
    )(yf, yb, bonus, gate, x, mod, gains, lnx, w_o, ones_bd)


def _rope_tables(t_len):
    rows = t_len // GRID_W
    row = jnp.repeat(jnp.arange(rows), GRID_W).astype(F32)
    col = jnp.tile(jnp.arange(GRID_W), rows).astype(F32)
    inv = 1.0 / (ROPE_THETA ** (jnp.arange(0, AXIS_DIM, 2, dtype=F32) / AXIS_DIM))
    ar = row[:, None] * inv[None, :]
    ac = col[:, None] * inv[None, :]
    ang = jnp.concatenate([ar, ar, ac, ac], axis=-1)
    cos, sin = jnp.cos(ang), jnp.sin(ang)
    first = (jnp.arange(HEAD_DIM) % AXIS_DIM) < AXIS_DIM // 2
    sin_a = jnp.where(first, -sin, 0.0)
    sin_b = jnp.where(first, 0.0, sin)
    rep = LANES // HEAD_DIM
    return tuple(jnp.tile(t, (1, rep)) for t in (cos, sin_a, sin_b))


def _dup_heads(t):
    lead = t.shape[:-1]
    t = t.reshape(lead + (GQA_KV_HEADS, 1, HEAD_DIM))
    return jnp.broadcast_to(t, lead + (GQA_KV_HEADS, LANES // HEAD_DIM, HEAD_DIM)).reshape(lead + (GQA_KV_HEADS * LANES,))


def _block_diag_pairs(s):
    b, lr = s.shape[:2]
    s = s.reshape(b, lr, 2 * N_PAIRS, 2, HEAD_DIM, HEAD_DIM)
    z = jnp.zeros_like(s[:, :, :, 0])
    top = jnp.concatenate([s[:, :, :, 0], z], axis=-1)
    bot = jnp.concatenate([z, s[:, :, :, 1]], axis=-1)
    return jnp.concatenate([top, bot], axis=-2)


def _dir_block_diag(w):
    z = jnp.zeros_like(w[:, 0])
    return jnp.concatenate([jnp.concatenate([w[:, 0], z], axis=-1), jnp.concatenate([z, w[:, 1]], axis=-1)], axis=-2)


def kernel(x_prompt, x_sample, c, cache_k_gqa, cache_v_gqa, cache_k_diff, cache_v_diff, state_rwkv, c_ctx, w_ada, b_ada, norm_gains, attn_w_in, attn_w_out, attn_qk_gain, diff_lambda, diff_subln, rwkv_mu, rwkv_w_rkv, rwkv_w_o, rwkv_w0, rwkv_w1, rwkv_w2, rwkv_a0, rwkv_a1, rwkv_a2, rwkv_g1, rwkv_g2, rwkv_kvec, rwkv_lnx, mlp_w1, mlp_w2):
    d = D_MODEL
    nb, seq, _ = x_prompt.shape
    nd, dseq, _ = x_sample.shape
    past = cache_k_gqa.shape[2]
    la = attn_w_in.shape[0]
    cond = jnp.concatenate([c_ctx[None, :], c, jnp.zeros((COND_ROWS - 1 - nd, d), F32)], axis=0)
    mod = _adaln(cond, w_ada, b_ada).reshape(DEPTH * COND_ROWS, 1, 6 * d)
    hd_ones = np.kron(np.eye(GQA_Q // HEAD_DIM, dtype=np.float32), np.ones((HEAD_DIM, HEAD_DIM), np.float32))
    ones_bd = jnp.asarray(hd_ones, BF16)
    rope_tabs = _rope_tables(dseq)
    w_in = attn_w_in.astype(BF16)
    w_out = attn_w_out.astype(BF16)
    gq = jnp.tile(attn_qk_gain[:, 0:1, :], (1, 1, GQA_HEADS))
    gk = jnp.tile(attn_qk_gain[:, 1:2, :], (1, 1, GQA_KV_HEADS))
    subln = diff_subln[:, None, :]
    cache_kv = (_dup_heads(cache_k_gqa.reshape(nd, la, past, GQA_KV)).astype(BF16),
                _dup_heads(cache_v_gqa.reshape(nd, la, past, GQA_KV)).astype(BF16),
                cache_k_diff.reshape(nd, la, past, DIFF_QK).astype(BF16),
                cache_v_diff.reshape(nd, la, past, DIFF_V).astype(BF16))
    rwkv_consts = (rwkv_mu, rwkv_w_rkv.astype(BF16),
                   jnp.stack([jnp.concatenate([rwkv_w1[:, 0], rwkv_w1[:, 1]], axis=-1),
                              jnp.concatenate([rwkv_a1[:, 0], rwkv_a1[:, 1]], axis=-1), rwkv_g1], axis=1).astype(BF16),
                   _dir_block_diag(rwkv_w2).astype(BF16), _dir_block_diag(rwkv_a2).astype(BF16),
                   rwkv_g2.astype(BF16), rwkv_w0, rwkv_a0, rwkv_kvec)
    w_o = rwkv_w_o.astype(BF16)
    s0_bd = _block_diag_pairs(state_rwkv)
    w1 = mlp_w1.astype(BF16)
    w2 = mlp_w2.astype(BF16)

    xp, xs = x_prompt, x_sample
    kg, vg, kdf, vdf, st = [], [], [], [], []
    for i in range(DEPTH):
        j = i // 2
        if i % 2 == 0:
            lam_init = 0.8 - 0.6 * math.exp(-0.3 * i)
            qa, qb, kdup, vdup, kb, vb, ka_f, va_f, kb_f, vb_f = _attn_in(
                xp, mod, i, 0, 0, norm_gains, j, w_in, gq, gk, ones_bd, None, True)
            kg.append(ka_f.reshape(nb, seq, GQA_KV_HEADS, HEAD_DIM))
            vg.append(va_f.reshape(nb, seq, GQA_KV_HEADS, HEAD_DIM))
            kdf.append(kb_f.reshape(nb, seq, DIFF_HEADS, 2, HEAD_DIM))
            vdf.append(vb_f.reshape(nb, seq, DIFF_HEADS, 2 * HEAD_DIM))
            mix = _attention(qa, qb, (kdup, vdup, kb, vb), None, j, diff_lambda, subln, lam_init)
            xp = _out_proj(mix, xp, mod, i, 0, 0, norm_gains, j, w_out)
            qa, qb, kdup, vdup, kb, vb = _attn_in(xs, mod, i, 1, 1, norm_gains, j, w_in, gq, gk, ones_bd, rope_tabs, False)
            mix = _attention(qa, qb, (kdup, vdup, kb, vb), cache_kv, j, diff_lambda, subln, lam_init)
            xs = _out_proj(mix, xs, mod, i, 1, 1, norm_gains, j, w_out)
        else:
            new = []
            for x, row0 in ((xp, 0), (xs, 1)):
                sample = row0 == 1
                r, v, kk, gate, bonus, lw, kd, ka = _rwkv_in(x, mod, i, row0, row0, norm_gains, j, rwkv_consts, ones_bd,
                                                             x.shape[1])
                res = _wkv(r, v, kk, lw, kd, ka, s0_bd if sample else None, j, want_final=not sample)
                if not sample:
                    st.append(res[2])
                new.append(_rwkv_out(res[0], res[1], bonus, gate, x, mod, i, row0, row0, norm_gains, j, rwkv_lnx, w_o,
                                     ones_bd))
            xp, xs = new
        xp = _mlp(xp.reshape(nb * seq // 512, 512, d), mod, i, 0, 0, norm_gains, w1, w2).reshape(nb, seq, d)
        xs = _mlp(xs, mod, i, 1, 1, norm_gains, w1, w2)
    return (xp, xs, jnp.stack(kg, axis=1), jnp.stack(vg, axis=1), jnp.stack(kdf, axis=1), jnp.stack(vdf, axis=1),
            jnp.stack(st, axis=1))
```

```python
import functools
import math

import numpy as np
import jax
import jax.numpy as jnp
from jax import lax
from jax.experimental import pallas as pl
from jax.experimental.pallas import tpu as pltpu

F32 = jnp.float32
BF16 = jnp.bfloat16

D_MODEL = 1024
DEPTH = 4
GRID_W = 64
HEAD_DIM = 64
AXIS_DIM = HEAD_DIM // 2
ROPE_THETA = 10000.0
GQA_HEADS = 8
GQA_KV_HEADS = 2
DIFF_HEADS = 4
RWKV_HEADS = D_MODEL // HEAD_DIM
D_FF = 4 * D_MODEL
GQA_Q = GQA_HEADS * HEAD_DIM
GQA_KV = GQA_KV_HEADS * HEAD_DIM
DIFF_QK = DIFF_HEADS * 2 * HEAD_DIM
DIFF_V = DIFF_HEADS * 2 * HEAD_DIM
ATTN_IN = GQA_Q + 2 * GQA_KV + 2 * DIFF_QK + DIFF_V
NORM_EPS = 1e-6
LNX_EPS = 64e-5

LANES = 128
SUBLANES = 8
N_PAIRS = D_MODEL // LANES
CHUNK = 64
COND_ROWS = 8
VMEM_LIMIT = 56 * 1024 * 1024


def _dot(a, b):
    return jnp.dot(a.astype(BF16), b.astype(BF16), preferred_element_type=F32)


def _dot_nt(a, b):
    return lax.dot_general(a.astype(BF16), b.astype(BF16), (((1,), (1,)), ((), ())), preferred_element_type=F32)


def _dot_tn(a, b):
    return lax.dot_general(a.astype(BF16), b.astype(BF16), (((0,), (0,)), ((), ())), preferred_element_type=F32)


def _bdot(a, b):
    return lax.dot_general(a.astype(BF16), b.astype(BF16), (((2,), (1,)), ((0,), (0,))), preferred_element_type=F32)


def _bdot_nt(a, b):
    return lax.dot_general(a.astype(BF16), b.astype(BF16), (((2,), (2,)), ((0,), (0,))), preferred_element_type=F32)


def _bdot_tn(a, b):
    return jnp.stack([_dot_tn(a[g], b[g]) for g in range(a.shape[0])])


def _rms(x):
    return x * lax.rsqrt(jnp.mean(x * x, axis=-1, keepdims=True) + NORM_EPS)


def _sigmoid(x):
    return 0.5 * jnp.tanh(0.5 * x) + 0.5


def _params(*sem):
    return pltpu.CompilerParams(dimension_semantics=sem, vmem_limit_bytes=VMEM_LIMIT)


def _full(arr):
    nd = arr.ndim
    return pl.BlockSpec(arr.shape, lambda *_: (0,) * nd, pipeline_mode=pl.Buffered(1))


def _layer(arr, j):
    nd = arr.ndim
    return pl.BlockSpec((1,) + arr.shape[1:], lambda *_: (j,) + (0,) * (nd - 1), pipeline_mode=pl.Buffered(1))


def _mod_spec(layer, row0, rowstride):
    return pl.BlockSpec((1, 1, 6 * D_MODEL), lambda b, i: (layer * COND_ROWS + row0 + b * rowstride, 0, 0))


def _tok(tm, w):
    return pl.BlockSpec((1, tm, w), lambda b, i: (b, i, 0))


def _adaln_kernel(c_ref, w_ref, b_ref, o_ref):
    c = c_ref[...]
    o_ref[0] = _dot(c * _sigmoid(c), w_ref[0]) + b_ref[0]


def _adaln(cond, w_ada, b_ada):
    tn = 1536
    return pl.pallas_call(
        _adaln_kernel,
        grid=(DEPTH, 6 * D_MODEL // tn),
        in_specs=[_full(cond),
                  pl.BlockSpec((1, D_MODEL, tn), lambda l, j: (l, 0, j)),
                  pl.BlockSpec((1, 1, tn), lambda l, j: (l, 0, j))],
        out_specs=pl.BlockSpec((1, COND_ROWS, tn), lambda l, j: (l, 0, j)),
        out_shape=jax.ShapeDtypeStruct((DEPTH, COND_ROWS, 6 * D_MODEL), F32),
        compiler_params=_params("parallel", "parallel"),
        name="adaln",
    )(cond, w_ada, b_ada.reshape(DEPTH, 1, 6 * D_MODEL))


def _head_ms(x, ones_bd):
    w = x.shape[-1]
    return _dot(x * x, ones_bd[0:w, 0:w]) * (1.0 / HEAD_DIM)


def _rope(x, cos, sin_a, sin_b):
    cols = []
    for j in range(x.shape[-1] // LANES):
        xj = x[:, j * LANES:(j + 1) * LANES]
        cols.append(xj * cos + pltpu.roll(xj, LANES - AXIS_DIM // 2, 1) * sin_a + pltpu.roll(xj, AXIS_DIM // 2, 1) * sin_b)
    return cols[0] if len(cols) == 1 else jnp.concatenate(cols, axis=-1)


def _dup_kv_heads(t, lo):
    swapped = pltpu.roll(t, HEAD_DIM, 1)
    return jnp.concatenate([jnp.where(lo, t, swapped), jnp.where(lo, swapped, t)], axis=-1)


def _attn_in_kernel(*refs, rope, side):
    x_ref, mod_ref, g_ref, w_ref, gq_ref, gk_ref, ones_ref = refs[:7]
    rest = list(refs[7:])
    tabs = [rest.pop(0) for _ in range(3)] if rope else None
    qa_ref, qb_ref, kdup_ref, vdup_ref, kb_ref, vb_ref = rest[:6]
    d = D_MODEL
    mod = mod_ref[0]
    h = _rms(x_ref[0]) * g_ref[0, 0:1, :] * (1.0 + mod[:, d:2 * d]) + mod[:, 0:d]
    z = _dot(h, w_ref[0])
    o = 0
    qa = z[:, o:o + GQA_Q]; o += GQA_Q
    ka = z[:, o:o + GQA_KV]; o += GQA_KV
    va = z[:, o:o + GQA_KV]; o += GQA_KV
    qb = z[:, o:o + DIFF_QK]; o += DIFF_QK
    kb = z[:, o:o + DIFF_QK]; o += DIFF_QK
    vb = z[:, o:o + DIFF_V]
    ones_bd = ones_ref[...]
    qa = qa * lax.rsqrt(_head_ms(qa, ones_bd) + NORM_EPS) * gq_ref[0]
    ka = ka * lax.rsqrt(_head_ms(ka, ones_bd) + NORM_EPS) * gk_ref[0]
    if side:
        ka_ref, va_ref, kbf_ref, vbf_ref = rest[6:]
        ka_ref[0] = ka
        va_ref[0] = va
        kbf_ref[0] = kb
        vbf_ref[0] = vb
    if rope:
        cos, sa, sb = (t[...] for t in tabs)
        qa, ka, qb, kb = (_rope(t, cos, sa, sb) for t in (qa, ka, qb, kb))
    scale = HEAD_DIM ** -0.5 * math.log2(math.e)
    lo = lax.broadcasted_iota(jnp.int32, (1, LANES), 1) < HEAD_DIM
    qa_ref[0] = (qa * scale).astype(BF16)
    qb_ref[0] = (qb * scale).astype(BF16)
    kdup_ref[0] = _dup_kv_heads(ka, lo).astype(BF16)
    vdup_ref[0] = _dup_kv_heads(va, lo).astype(BF16)
    kb_ref[0] = kb.astype(BF16)
    vb_ref[0] = vb.astype(BF16)


def _attn_in(x, mod, layer, row0, rowstride, gains, j, w_in, gq, gk, ones_bd, rope_tabs, side, tm=256):
    bx, tx, d = x.shape
    widths = (GQA_Q, DIFF_QK, 2 * LANES, 2 * LANES, DIFF_QK, DIFF_V)
    in_specs = [_tok(tm, d), _mod_spec(layer, row0, rowstride), _layer(gains, layer), _layer(w_in, j), _layer(gq, j),
                _layer(gk, j), _full(ones_bd)]
    args = [x, mod, gains, w_in, gq, gk, ones_bd]
    if rope_tabs is not None:
        in_specs += [pl.BlockSpec((tm, LANES), lambda b, i: (i, 0))] * 3
        args += list(rope_tabs)
    out_specs = [_tok(tm, w) for w in widths]
    out_shape = [jax.ShapeDtypeStruct((bx, tx, w), BF16) for w in widths]
    if side:
        side_widths = (GQA_KV, GQA_KV, DIFF_QK, DIFF_V)
        out_specs += [_tok(tm, w) for w in side_widths]
        out_shape += [jax.ShapeDtypeStruct((bx, tx, w), F32) for w in side_widths]
    return pl.pallas_call(
        functools.partial(_attn_in_kernel, rope=rope_tabs is not None, side=side),
        grid=(bx, tx // tm),
        in_specs=in_specs,
        out_specs=out_specs,
        out_shape=out_shape,
        compiler_params=_params("parallel", "parallel"),
        name="attn_in",
    )(*args)


def _softmax_parts(q, ks):
    ss = [_dot_nt(q, k) for k in ks]
    m = functools.reduce(jnp.maximum, [jnp.max(s, axis=-1, keepdims=True) for s in ss])
    ps = [jnp.exp2(s - m) for s in ss]
    return ps, sum(jnp.sum(p, axis=-1, keepdims=True) for p in ps)


def _attn_kernel(*refs, lam_init, cached):
    qa_ref, qb_ref = refs[:2]
    n_seg = 2 if cached else 1
    segs = refs[2:2 + 4 * n_seg]
    ka_refs, va_refs, kb_refs, vb_refs = (segs[i::4] for i in range(4))
    lam_ref, sub_ref, o_ref = refs[2 + 4 * n_seg:]
    lo = lax.broadcasted_iota(jnp.int32, (1, LANES), 1) < HEAD_DIM
    hi = jnp.logical_not(lo)
    for g in range(GQA_KV_HEADS):
        ks = [r[0, :, g * LANES:(g + 1) * LANES] for r in ka_refs]
        vs = [r[0, :, g * LANES:(g + 1) * LANES] for r in va_refs]
        for j in range(GQA_HEADS // GQA_KV_HEADS // 2):
            c0 = (g * (GQA_HEADS // GQA_KV_HEADS // 2) + j) * LANES
            q = qa_ref[0, :, c0:c0 + LANES]
            halves = []
            for keep in (lo, hi):
                ps, total = _softmax_parts(jnp.where(keep, q, jnp.zeros_like(q)), ks)
                halves.append(sum(_dot(p, v) for p, v in zip(ps, vs)) * (1.0 / total))
            o_ref[0, :, c0:c0 + LANES] = jnp.where(lo, halves[0], halves[1])
    lam4 = lam_ref[0]
    lam = (jnp.exp(jnp.sum(lam4[0:1] * lam4[1:2], axis=-1, keepdims=True))
           - jnp.exp(jnp.sum(lam4[2:3] * lam4[3:4], axis=-1, keepdims=True)) + lam_init)
    for hd in range(DIFF_HEADS):
        c0 = hd * LANES
        q = qb_ref[0, :, c0:c0 + LANES]
        ks = [r[0, :, c0:c0 + LANES] for r in kb_refs]
        vs = [r[0, :, c0:c0 + LANES] for r in vb_refs]
        p0, total0 = _softmax_parts(jnp.where(lo, q, jnp.zeros_like(q)), ks)
        p1, total1 = _softmax_parts(jnp.where(hi, q, jnp.zeros_like(q)), ks)
        ratio = lam * total0 / total1
        o = sum(_dot(a - ratio * b, v) for a, b, v in zip(p0, p1, vs)) * (1.0 / total0)
        o_ref[0, :, GQA_Q + c0:GQA_Q + c0 + LANES] = _rms(o) * sub_ref[0] * (1.0 - lam_init)


def _attention(qa, qb, new_kv, cache_kv, j, lam, subln, lam_init, tq=256):
    bx, tx, _ = qa.shape
    seg_specs, seg_args = [], []
    for t in new_kv:
        seg_specs.append(pl.BlockSpec((1, tx, t.shape[-1]), lambda b, i: (b, 0, 0)))
        seg_args.append(t)
    if cache_kv is not None:
        for t in cache_kv:
            seg_specs.append(pl.BlockSpec((1, None) + t.shape[2:], lambda b, i: (b, j, 0, 0)))
            seg_args.append(t)
    return pl.pallas_call(
        functools.partial(_attn_kernel, lam_init=lam_init, cached=cache_kv is not None),
        grid=(bx, tx // tq),
        in_specs=[_tok(tq, GQA_Q), _tok(tq, DIFF_QK)] + seg_specs + [_layer(lam, j), _layer(subln, j)],
        out_specs=_tok(tq, D_MODEL),
        out_shape=jax.ShapeDtypeStruct((bx, tx, D_MODEL), F32),
        compiler_params=_params("parallel", "parallel"),
        name="attention",
    )(qa, qb, *seg_args, lam, subln)


MLP_CHUNK = 512


def _mixer_residual(x, mixed, w_out, mod, gains):
    d = D_MODEL
    return x + mod[:, 2 * d:3 * d] * (_rms(_dot(mixed, w_out)) * gains[1:2, :])


def _mlp_residual(x, mod, gains, w1_ref, w2_ref):
    d = D_MODEL
    h = (_rms(x) * gains[2:3, :] * (1.0 + mod[:, 4 * d:5 * d]) + mod[:, 3 * d:4 * d]).astype(BF16)
    acc = jnp.zeros(x.shape, F32)
    for c in range(0, D_FF, MLP_CHUNK):
        u = jnp.maximum(jnp.dot(h, w1_ref[0, :, c:c + MLP_CHUNK], preferred_element_type=F32), 0.0)
        acc = acc + _dot(u * u, w2_ref[0, c:c + MLP_CHUNK, :])
    return x + mod[:, 5 * d:6 * d] * (_rms(acc) * gains[3:4, :])


TAIL_ROWS = 512


def _tail_tiling(bx, tx):
    rows = min(TAIL_ROWS, tx)
    nbb = TAIL_ROWS // rows
    return nbb, rows, (bx // nbb, tx // rows)


def _attn_tail_kernel(m_ref, x_ref, mod_ref, g_ref, wo_ref, w1_ref, w2_ref, o_ref):
    mod, gains = mod_ref[0], g_ref[0]
    flat = (TAIL_ROWS, D_MODEL)
    x = _mixer_residual(x_ref[...].reshape(flat), m_ref[...].reshape(flat), wo_ref[0], mod, gains)
    o_ref[...] = _mlp_residual(x, mod, gains, w1_ref, w2_ref).reshape(o_ref.shape)


def _attn_tail(mix, x, mod, layer, row0, rowstride, gains, j, w_out, w1, w2):
    bx, tx, d = x.shape
    nbb, rows, grid = _tail_tiling(bx, tx)
    tok = pl.BlockSpec((nbb, rows, d), lambda b, i: (b, i, 0))
    return pl.pallas_call(
        _attn_tail_kernel,
        grid=grid,
        in_specs=[tok, tok, _mod_spec(layer, row0, rowstride), _layer(gains, layer), _layer(w_out, j),
                  _layer(w1, layer), _layer(w2, layer)],
        out_specs=tok,
        out_shape=jax.ShapeDtypeStruct(x.shape, F32),
        compiler_params=_params("parallel", "parallel"),
        name="attn_tail",
    )(mix, x, mod, gains, w_out, w1, w2)


def _head_sum_signed(t, ones_bd):
    w = 2 * LANES
    hi = t.astype(BF16)
    return jnp.concatenate(
        [jnp.dot(hi[:, j:j + w], ones_bd[0:w, 0:w], preferred_element_type=F32)
         + _dot(t[:, j:j + w] - hi[:, j:j + w].astype(F32), ones_bd[0:w, 0:w]) for j in range(0, t.shape[-1], w)],
        axis=-1)


def _head_ms_wide(x, ones_bd):
    w = 2 * LANES
    return jnp.concatenate([_dot(x[:, j:j + w] * x[:, j:j + w], ones_bd[0:w, 0:w]) for j in range(0, x.shape[-1], w)],
                           axis=-1) * (1.0 / HEAD_DIM)


def _rwkv_in_kernel(x_ref, xp_ref, xn_ref, mod_ref, g_ref, mu_ref, wrkv_ref, wlo_ref, w2_ref, a2_ref, g2_ref,
                    w0_ref, a0_ref, kv_ref, ones_ref,
                    r_ref, v_ref, kk_ref, gate_ref, bonus_ref, lw_ref, kd_ref, ka_ref, hs_ref, *, seq_len):
    d = D_MODEL
    tm = x_ref.shape[1]
    mod = mod_ref[0]
    g0 = g_ref[0, 0:1, :]
    sc = 1.0 + mod[:, d:2 * d]
    sh = mod[:, 0:d]
    norm = lambda t: _rms(t) * g0 * sc + sh
    h = norm(x_ref[0])
    hs_ref[SUBLANES:SUBLANES + tm, :] = h
    hs_ref[0:SUBLANES, :] = norm(xp_ref[0])
    hs_ref[SUBLANES + tm:2 * SUBLANES + tm, :] = norm(xn_ref[0])
    row = lax.broadcasted_iota(jnp.int32, (tm, 1), 0)
    t_seq = (pl.program_id(1) * tm + row) % seq_len
    prev = jnp.where(t_seq == 0, 0.0, hs_ref[SUBLANES - 1:SUBLANES - 1 + tm, :])
    nxt = jnp.where(t_seq == seq_len - 1, 0.0, hs_ref[SUBLANES + 1:SUBLANES + 1 + tm, :])
    xx = 0.5 * (prev + nxt) - h
    xr, xw, xk, xv, xa, xg = (h + xx * mu_ref[0, n:n + 1, :] for n in range(6))
    r = _dot(xr, wrkv_ref[0, 0])
    k = _dot(xk, wrkv_ref[0, 1])
    v = _dot(xv, wrkv_ref[0, 2])
    gate = _dot(_sigmoid(_dot(xg, wlo_ref[0, 2])), g2_ref[0])
    wl = _dot(jnp.tanh(_dot(xw, wlo_ref[0, 0])), w2_ref[0])
    al = _dot(_dot(xa, wlo_ref[0, 1]), a2_ref[0])
    ones_bd = ones_ref[...]
    k_k, k_a, r_k = kv_ref[0, 0:1, :], kv_ref[0, 1:2, :], kv_ref[0, 2:3, :]
    kk = k * k_k
    kk = kk * lax.rsqrt(_head_ms_wide(kk, ones_bd) * HEAD_DIM + 1e-12)
    bonus = jnp.zeros_like(v)
    for dr in range(2):
        lw = (-math.exp(-0.5)) * _sigmoid(w0_ref[0, dr:dr + 1, :] + wl[:, dr * d:(dr + 1) * d])
        a = _sigmoid(a0_ref[0, dr:dr + 1, :] + al[:, dr * d:(dr + 1) * d])
        kd = k * (1.0 + (a - 1.0) * k_a)
        ka = kk * a
        bonus = bonus + _head_sum_signed(r * kd * r_k, ones_bd) * v
        for p in range(N_PAIRS):
            sl = slice(p * LANES, (p + 1) * LANES)
            lw_ref[dr, 0, p] = lw[:, sl]
            kd_ref[dr, 0, p] = kd[:, sl]
            ka_ref[dr, 0, p] = ka[:, sl]
    for p in range(N_PAIRS):
        sl = slice(p * LANES, (p + 1) * LANES)
        r_ref[0, p] = r[:, sl]
        v_ref[0, p] = v[:, sl]
        kk_ref[0, p] = kk[:, sl]
    gate_ref[0] = gate
    bonus_ref[0] = bonus


def _rwkv_in(x, mod, layer, row0, rowstride, gains, j, consts, ones_bd, seq_len, tm=256):
    bx, tx, d = x.shape
    r8 = tm // SUBLANES
    pair = jax.ShapeDtypeStruct((bx, N_PAIRS, tx, LANES), F32)
    pair2 = jax.ShapeDtypeStruct((2, bx, N_PAIRS, tx, LANES), F32)
    tok = jax.ShapeDtypeStruct((bx, tx, d), F32)
    pair_spec = pl.BlockSpec((1, N_PAIRS, tm, LANES), lambda b, i: (b, 0, i, 0))
    pair2_spec = pl.BlockSpec((2, 1, N_PAIRS, tm, LANES), lambda b, i: (0, b, 0, i, 0))
    return pl.pallas_call(
        functools.partial(_rwkv_in_kernel, seq_len=seq_len),
        grid=(bx, tx // tm),
        in_specs=[_tok(tm, d),
                  pl.BlockSpec((1, SUBLANES, d), lambda b, i: (b, jnp.maximum(i * r8 - 1, 0), 0)),
                  pl.BlockSpec((1, SUBLANES, d), lambda b, i: (b, jnp.minimum((i + 1) * r8, tx // SUBLANES - 1), 0)),
                  _mod_spec(layer, row0, rowstride), _layer(gains, layer)]
                 + [_layer(c, j) for c in consts] + [_full(ones_bd)],
        out_specs=[pair_spec, pair_spec, pair_spec, _tok(tm, d), _tok(tm, d), pair2_spec, pair2_spec, pair2_spec],
        out_shape=[pair, pair, pair, tok, tok, pair2, pair2, pair2],
        scratch_shapes=[pltpu.VMEM((tm + 2 * SUBLANES, d), F32)],
        compiler_params=_params("parallel", "parallel"),
        name="rwkv_in",
    )(x, x, x, mod, gains, *consts, ones_bd)


def _wkv_masks():
    t = np.arange(CHUNK)
    order64, half64, tri, lvl = [], [], [], []
    for rev in (False, True):
        pos = (CHUNK - 1 - t) if rev else t
        pt, ps = pos[:, None], np.tile(pos, 2)[None, :]
        tri.append(np.concatenate([ps < pt, ps <= pt], axis=0))
        mats = []
        n = 1
        while n < CHUNK:
            mats.append((pt // (2 * n) == ps // (2 * n)) & ((pt // n) % 2 == 1) & ((ps // n) % 2 == 0))
            n *= 2
        lvl.append(np.stack(mats))
        order64.append(pos[None, :] <= pos[:, None])
        half64.append((pos < CHUNK // 2)[:, None])
    as32 = lambda m: np.stack(m).astype(np.float32)
    eye = np.tile(np.eye(CHUNK, dtype=np.float32), (1, 2))
    same_head = np.kron(np.eye(2, dtype=np.float32), np.ones((HEAD_DIM, HEAD_DIM), np.float32))
    return as32(order64), as32(half64), as32(tri), as32(lvl), eye, same_head


def _wkv_chunk(r, v, kk, lw, kd, ka, s_bd, order, half, tri, lvl, eye, same_head):
    g = 2 * N_PAIRS
    n = CHUNK

    def per_dir(x, m):
        return (x.reshape((2, N_PAIRS) + x.shape[1:]) * m[:, None]).reshape(x.shape)

    lo = lax.broadcasted_iota(jnp.int32, (1, 1, LANES), 2) < HEAD_DIM
    stack = lambda t: jnp.concatenate([jnp.where(lo, t, 0.0), jnp.where(lo, 0.0, t)], axis=1)
    order_g = jnp.broadcast_to(order[:, None], (2, N_PAIRS, n, n)).reshape(g, n, n).astype(BF16)
    lw_hi = lw.astype(BF16)
    cum = _bdot(order_g, lw_hi) + _bdot(order_g, lw - lw_hi.astype(F32))
    tot = jnp.sum(lw, axis=1, keepdims=True)
    mid = jnp.sum(per_dir(lw, half), axis=1, keepdims=True)
    g_inv = jnp.exp(mid - cum)
    g_tail = jnp.exp(tot - cum)
    reads = jnp.concatenate([kk * jnp.exp(cum - lw - mid), r * jnp.exp(cum - mid)], axis=1)
    gram = _bdot_nt(reads, jnp.concatenate([stack(ka * g_inv), stack(kd * g_inv)], axis=1))
    ac = per_dir(gram[:, :, 0:2 * n], tri)
    bd = per_dir(gram[:, :, 2 * n:], tri)
    a = ac[:, 0:n]
    tinv = eye[None] - per_dir(a, lvl[:, 0])
    for level in range(1, lvl.shape[1]):
        w = _bdot(per_dir(a, lvl[:, level]), stack(tinv))
        tinv = tinv - _bdot(tinv, stack(w))
    from_state = _bdot_nt(reads, s_bd * jnp.exp(mid))
    from_v = _bdot(bd, stack(v))
    u = _bdot(tinv, stack(from_state[:, 0:n] + from_v[:, 0:n]))
    y = from_state[:, n:] + from_v[:, n:] - _bdot(ac[:, n:], stack(u))
    outer = _bdot_tn(jnp.concatenate([v, u], axis=1), jnp.concatenate([kd * g_tail, -(ka * g_tail)], axis=1))
    return y, s_bd * jnp.exp(tot) + outer * same_head[None]


def _wkv_kernel(*refs, has_init, want_final):
    (rf_ref, vf_ref, kkf_ref, rb_ref, vb_ref, kkb_ref, lwf_ref, kdf_ref, kaf_ref, lwb_ref, kdb_ref, kab_ref,
     order_ref, half_ref, tri_ref, lvl_ref, eye_ref, same_ref) = refs[:18]
    rest = list(refs[18:])
    s0_ref = rest.pop(0) if has_init else None
    yf_ref = rest.pop(0)
    yb_ref = rest.pop(0)
    sf_ref = rest.pop(0) if want_final else None
    s_scr = rest.pop(0)
    c = pl.program_id(1)
    nc = pl.num_programs(1)

    @pl.when(c == 0)
    def _():
        if has_init:
            s_scr[...] = s0_ref[0, 0]
        else:
            s_scr[...] = jnp.zeros(s_scr.shape, F32)

    both = lambda f_ref, b_ref: jnp.concatenate([f_ref[...].reshape(N_PAIRS, CHUNK, LANES),
                                                 b_ref[...].reshape(N_PAIRS, CHUNK, LANES)], axis=0)
    y, s_new = _wkv_chunk(both(rf_ref, rb_ref), both(vf_ref, vb_ref), both(kkf_ref, kkb_ref), both(lwf_ref, lwb_ref),
                          both(kdf_ref, kdb_ref), both(kaf_ref, kab_ref), s_scr[...],
                          order_ref[...], half_ref[...], tri_ref[...], lvl_ref[...], eye_ref[...], same_ref[...])
    yf_ref[0] = y[0:N_PAIRS]
    yb_ref[0] = y[N_PAIRS:]
    s_scr[...] = s_new

    if want_final:
        @pl.when(c == nc - 1)
        def _():
            for dr in range(2):
                for p in range(N_PAIRS):
                    s = s_scr[dr * N_PAIRS + p]
                    sf_ref[0, dr, 2 * p] = s[0:HEAD_DIM, 0:HEAD_DIM]
                    sf_ref[0, dr, 2 * p + 1] = s[HEAD_DIM:, HEAD_DIM:]


def _wkv(r, v, kk, lw, kd, ka, s0_bd, j, want_final):
    bx, _, tx, _ = r.shape
    nc = tx // CHUNK
    masks = tuple(jnp.asarray(m) for m in _wkv_masks())
    fwd = pl.BlockSpec((1, N_PAIRS, CHUNK, LANES), lambda b, c: (b, 0, c, 0))
    bwd = pl.BlockSpec((1, N_PAIRS, CHUNK, LANES), lambda b, c: (b, 0, nc - 1 - c, 0))
    fwd2 = pl.BlockSpec((1, 1, N_PAIRS, CHUNK, LANES), lambda b, c: (0, b, 0, c, 0))
    bwd2 = pl.BlockSpec((1, 1, N_PAIRS, CHUNK, LANES), lambda b, c: (1, b, 0, nc - 1 - c, 0))
    in_specs = [fwd, fwd, fwd, bwd, bwd, bwd, fwd2, fwd2, fwd2, bwd2, bwd2, bwd2] + [_full(m) for m in masks]
    args = [r, v, kk, r, v, kk, lw, kd, ka, lw, kd, ka, *masks]
    if s0_bd is not None:
        in_specs.append(pl.BlockSpec((1, 1, 2 * N_PAIRS, LANES, LANES), lambda b, c: (b, j, 0, 0, 0)))
        args.append(s0_bd)
    y_shape = jax.ShapeDtypeStruct(r.shape, F32)
    out_specs = [fwd, bwd]
    out_shape = [y_shape, y_shape]
    if want_final:
        out_specs.append(pl.BlockSpec((1, 2, RWKV_HEADS, HEAD_DIM, HEAD_DIM), lambda b, c: (b, 0, 0, 0, 0)))
        out_shape.append(jax.ShapeDtypeStruct((bx, 2, RWKV_HEADS, HEAD_DIM, HEAD_DIM), F32))
    return pl.pallas_call(
        functools.partial(_wkv_kernel, has_init=s0_bd is not None, want_final=want_final),
        grid=(bx, nc),
        in_specs=in_specs,
        out_specs=out_specs,
        out_shape=out_shape,
        scratch_shapes=[pltpu.VMEM((2 * N_PAIRS, LANES, LANES), F32)],
        compiler_params=_params("parallel", "arbitrary"),
        name="wkv",
    )(*args)


def _rwkv_tail_kernel(yf_ref, yb_ref, bonus_ref, gate_ref, x_ref, mod_ref, g_ref, lnx_ref, wo_ref, ones_ref,
                      w1_ref, w2_ref, o_ref):
    ones_bd = ones_ref[...]
    mod, gains = mod_ref[0], g_ref[0]
    flat = (TAIL_ROWS, D_MODEL)
    y = jnp.concatenate([(yf_ref[:, p] + yb_ref[:, p]).reshape(TAIL_ROWS, LANES) for p in range(N_PAIRS)], axis=-1)
    yc = y - _head_sum_signed(y, ones_bd) * (1.0 / HEAD_DIM)
    yn = yc * lax.rsqrt(_head_ms_wide(yc, ones_bd) + LNX_EPS)
    z = (yn * lnx_ref[0, 0:1, :] + lnx_ref[0, 1:2, :] + bonus_ref[...].reshape(flat)) * gate_ref[...].reshape(flat)
    x = _mixer_residual(x_ref[...].reshape(flat), z, wo_ref[0], mod, gains)
    o_ref[...] = _mlp_residual(x, mod, gains, w1_ref, w2_ref).reshape(o_ref.shape)


def _rwkv_tail(yf, yb, bonus, gate, x, mod, layer, row0, rowstride, gains, j, lnx, w_o, ones_bd, w1, w2):
    bx, tx, d = x.shape
    nbb, rows, grid = _tail_tiling(bx, tx)
    tok = pl.BlockSpec((nbb, rows, d), lambda b, i: (b, i, 0))
    pair = pl.BlockSpec((nbb, N_PAIRS, rows, LANES), lambda b, i: (b, 0, i, 0))
    return pl.pallas_call(
        _rwkv_tail_kernel,
        grid=grid,
        in_specs=[pair, pair, tok, tok, tok, _mod_spec(layer, row0, rowstride),
                  _layer(gains, layer), _layer(lnx, j), _layer(w_o, j), _full(ones_bd), _layer(w1, layer),
                  _layer(w2, layer)],
        out_specs=tok,
        out_shape=jax.ShapeDtypeStruct(x.shape, F32),
        compiler_params=_params("parallel", "parallel"),
        name="rwkv_tail",
    )(yf, yb, bonus, gate, x, mod, gains, lnx, w_o, ones_bd, w1, w2)


def _rope_tables(t_len):
    rows = t_len // GRID_W
    row = jnp.repeat(jnp.arange(rows), GRID_W).astype(F32)
    col = jnp.tile(jnp.arange(GRID_W), rows).astype(F32)
    inv = 1.0 / (ROPE_THETA ** (jnp.arange(0, AXIS_DIM, 2, dtype=F32) / AXIS_DIM))
    ar = row[:, None] * inv[None, :]
    ac = col[:, None] * inv[None, :]
    ang = jnp.concatenate([ar, ar, ac, ac], axis=-1)
    cos, sin = jnp.cos(ang), jnp.sin(ang)
    first = (jnp.arange(HEAD_DIM) % AXIS_DIM) < AXIS_DIM // 2
    sin_a = jnp.where(first, -sin, 0.0)
    sin_b = jnp.where(first, 0.0, sin)
    rep = LANES // HEAD_DIM
    return tuple(jnp.tile(t, (1, rep)) for t in (cos, sin_a, sin_b))


def _dup_heads(t):
    lead = t.shape[:-1]
    t = t.reshape(lead + (GQA_KV_HEADS, 1, HEAD_DIM))
    return jnp.broadcast_to(t, lead + (GQA_KV_HEADS, LANES // HEAD_DIM, HEAD_DIM)).reshape(lead + (GQA_KV_HEADS * LANES,))


def _block_diag_pairs(s):
    b, lr = s.shape[:2]
    s = s.reshape(b, lr, 2 * N_PAIRS, 2, HEAD_DIM, HEAD_DIM)
    z = jnp.zeros_like(s[:, :, :, 0])
    top = jnp.concatenate([s[:, :, :, 0], z], axis=-1)
    bot = jnp.concatenate([z, s[:, :, :, 1]], axis=-1)
    return jnp.concatenate([top, bot], axis=-2)


def _dir_block_diag(w):
    z = jnp.zeros_like(w[:, 0])
    return jnp.concatenate([jnp.concatenate([w[:, 0], z], axis=-1), jnp.concatenate([z, w[:, 1]], axis=-1)], axis=-2)


def kernel(x_prompt, x_sample, c, cache_k_gqa, cache_v_gqa, cache_k_diff, cache_v_diff, state_rwkv, c_ctx, w_ada, b_ada, norm_gains, attn_w_in, attn_w_out, attn_qk_gain, diff_lambda, diff_subln, rwkv_mu, rwkv_w_rkv, rwkv_w_o, rwkv_w0, rwkv_w1, rwkv_w2, rwkv_a0, rwkv_a1, rwkv_a2, rwkv_g1, rwkv_g2, rwkv_kvec, rwkv_lnx, mlp_w1, mlp_w2):
    d = D_MODEL
    nb, seq, _ = x_prompt.shape
    nd, dseq, _ = x_sample.shape
    past = cache_k_gqa.shape[2]
    la = attn_w_in.shape[0]
    cond = jnp.concatenate([c_ctx[None, :], c, jnp.zeros((COND_ROWS - 1 - nd, d), F32)], axis=0)
    mod = _adaln(cond, w_ada, b_ada).reshape(DEPTH * COND_ROWS, 1, 6 * d)
    hd_ones = np.kron(np.eye(GQA_Q // HEAD_DIM, dtype=np.float32), np.ones((HEAD_DIM, HEAD_DIM), np.float32))
    ones_bd = jnp.asarray(hd_ones, BF16)
    rope_tabs = _rope_tables(dseq)
    w_in = attn_w_in.astype(BF16)
    w_out = attn_w_out.astype(BF16)
    gq = jnp.tile(attn_qk_gain[:, 0:1, :], (1, 1, GQA_HEADS))
    gk = jnp.tile(attn_qk_gain[:, 1:2, :], (1, 1, GQA_KV_HEADS))
    subln = diff_subln[:, None, :]
    cache_kv = (_dup_heads(cache_k_gqa.reshape(nd, la, past, GQA_KV)).astype(BF16),
                _dup_heads(cache_v_gqa.reshape(nd, la, past, GQA_KV)).astype(BF16),
                cache_k_diff.reshape(nd, la, past, DIFF_QK).astype(BF16),
                cache_v_diff.reshape(nd, la, past, DIFF_V).astype(BF16))
    rwkv_consts = (rwkv_mu, rwkv_w_rkv.astype(BF16),
                   jnp.stack([jnp.concatenate([rwkv_w1[:, 0], rwkv_w1[:, 1]], axis=-1),
                              jnp.concatenate([rwkv_a1[:, 0], rwkv_a1[:, 1]], axis=-1), rwkv_g1], axis=1).astype(BF16),
                   _dir_block_diag(rwkv_w2).astype(BF16), _dir_block_diag(rwkv_a2).astype(BF16),
                   rwkv_g2.astype(BF16), rwkv_w0, rwkv_a0, rwkv_kvec)
    w_o = rwkv_w_o.astype(BF16)
    s0_bd = _block_diag_pairs(state_rwkv)
    w1 = mlp_w1.astype(BF16)
    w2 = mlp_w2.astype(BF16)

    xp, xs = x_prompt, x_sample
    kg, vg, kdf, vdf, st = [], [], [], [], []
    for i in range(DEPTH):
        j = i // 2
        if i % 2 == 0:
            lam_init = 0.8 - 0.6 * math.exp(-0.3 * i)
            qa, qb, kdup, vdup, kb, vb, ka_f, va_f, kb_f, vb_f = _attn_in(
                xp, mod, i, 0, 0, norm_gains, j, w_in, gq, gk, ones_bd, None, True)
            kg.append(ka_f.reshape(nb, seq, GQA_KV_HEADS, HEAD_DIM))
            vg.append(va_f.reshape(nb, seq, GQA_KV_HEADS, HEAD_DIM))
            kdf.append(kb_f.reshape(nb, seq, DIFF_HEADS, 2, HEAD_DIM))
            vdf.append(vb_f.reshape(nb, seq, DIFF_HEADS, 2 * HEAD_DIM))
            mix = _attention(qa, qb, (kdup, vdup, kb, vb), None, j, diff_lambda, subln, lam_init)
            xp = _attn_tail(mix, xp, mod, i, 0, 0, norm_gains, j, w_out, w1, w2)
            qa, qb, kdup, vdup, kb, vb = _attn_in(xs, mod, i, 1, 1, norm_gains, j, w_in, gq, gk, ones_bd, rope_tabs, False)
            mix = _attention(qa, qb, (kdup, vdup, kb, vb), cache_kv, j, diff_lambda, subln, lam_init)
            xs = _attn_tail(mix, xs, mod, i, 1, 1, norm_gains, j, w_out, w1, w2)
        else:
            new = []
            for x, row0 in ((xp, 0), (xs, 1)):
                sample = row0 == 1
                r, v, kk, gate, bonus, lw, kd, ka = _rwkv_in(x, mod, i, row0, row0, norm_gains, j, rwkv_consts, ones_bd,
                                                             x.shape[1])
                res = _wkv(r, v, kk, lw, kd, ka, s0_bd if sample else None, j, want_final=not sample)
                if not sample:
                    st.append(res[2])
                new.append(_rwkv_tail(res[0], res[1], bonus, gate, x, mod, i, row0, row0, norm_gains, j, rwkv_lnx, w_o,
                                      ones_bd, w1, w2))
            xp, xs = new
    return (xp, xs, jnp.stack(kg, axis=1), jnp.stack(vg, axis=1), jnp.stack(kdf, axis=1), jnp.stack(vdf, axis=1),
            jnp.stack(st, axis=1))
```
